```python
import math, functools
import jax, jax.numpy as jnp
from jax import lax
import numpy as np

D_MODEL = 1024
BATCH = 2
SEQ = 8192
DEPTH = 1
DEC_BATCH = 128
DEC_SEQ = 8
PAST_LEN = 8192
PAGE_SIZE = 128

D_MIX = D_MODEL
D_ATT = D_MIX // 2
HEAD_DIM = 64
N_HEADS = D_ATT // HEAD_DIM
D_CONV = D_MIX - D_ATT
CONV_WIDTH = 31
D_FF = 2816
FFN_CONV_WIDTH = 3
Q_BLOCK = 128
N_MOD = 6
EPS = 1e-6
D_IN = 3 * D_ATT + N_HEADS + 2 * D_CONV

kernel_name = 'hymba_fox_conformer_convffn_step'


def rms_norm(x, g):
    xf = x.astype(jnp.float32)
    y = xf * lax.rsqrt(jnp.mean(xf * xf, axis=-1, keepdims=True) + EPS)
    return (y * g.astype(jnp.float32)).astype(x.dtype)


def layer_norm(x, g, b):
    xf = x.astype(jnp.float32)
    mu = jnp.mean(xf, axis=-1, keepdims=True)
    var = jnp.mean(jnp.square(xf - mu), axis=-1, keepdims=True)
    y = (xf - mu) * lax.rsqrt(var + EPS)
    return (y * g.astype(jnp.float32) + b.astype(jnp.float32)).astype(x.dtype)


def causal_dwconv(x_ext, w, b):
    c = x_ext.shape[-1]
    y = lax.conv_general_dilated(x_ext, w[:, None, :].astype(x_ext.dtype), window_strides=(1,), padding='VALID',
                                 dimension_numbers=('NWC', 'WIO', 'NWC'), feature_group_count=c)
    return y + b


def fox_prompt(q, k, v, logf):
    b, t, h, dh = q.shape
    nb = t // Q_BLOCK
    scale = 1.0 / math.sqrt(dh)
    cum = jnp.cumsum(logf, axis=1)
    cum_t = cum.transpose(0, 2, 1)
    kf = k.astype(jnp.float32)
    vf = v.astype(jnp.float32)
    qb = q.reshape(b, nb, Q_BLOCK, h, dh).transpose(1, 0, 2, 3, 4)
    cb = cum.reshape(b, nb, Q_BLOCK, h).transpose(1, 0, 3, 2)
    key_pos = jnp.arange(t)

    def block(args):
        i, qi, ci = args
        s = jnp.einsum('bqhd,bkhd->bhqk', qi.astype(jnp.float32), kf) * scale
        s = s + ci[..., None] - cum_t[:, :, None, :]
        q_pos = i * Q_BLOCK + jnp.arange(Q_BLOCK)
        mask = key_pos[None, :] <= q_pos[:, None]
        s = jnp.where(mask, s, -jnp.inf)
        p = jax.nn.softmax(s, axis=-1)
        return jnp.einsum('bhqk,bkhd->bqhd', p, vf)

    o = lax.map(block, (jnp.arange(nb), qb, cb))
    return o.transpose(1, 0, 2, 3, 4).reshape(b, t, h * dh).astype(q.dtype)


def fox_sample(q, k_new, v_new, logf_new, cache_k, cache_v, cache_logf, page_table):
    b, t, h, dh = q.shape
    past = page_table.shape[1] * PAGE_SIZE
    scale = 1.0 / math.sqrt(dh)
    k_past = cache_k[page_table].reshape(b, past, h, dh).astype(jnp.float32)
    v_past = cache_v[page_table].reshape(b, past, h, dh).astype(jnp.float32)
    lf_past = cache_logf[page_table].reshape(b, past, h).astype(jnp.float32)
    rest = jnp.cumsum(lf_past[:, ::-1], axis=1)[:, ::-1] - lf_past
    cn = jnp.cumsum(logf_new, axis=1)
    cn_t = cn.transpose(0, 2, 1)
    qf = q.astype(jnp.float32)
    s_past = jnp.einsum('bqhd,bkhd->bhqk', qf, k_past) * scale
    s_past = s_past + cn_t[..., None] + rest.transpose(0, 2, 1)[:, :, None, :]
    s_new = jnp.einsum('bqhd,bkhd->bhqk', qf, k_new.astype(jnp.float32)) * scale
    s_new = s_new + cn_t[..., None] - cn_t[:, :, None, :]
    causal = jnp.arange(t)[None, :] <= jnp.arange(t)[:, None]
    s_new = jnp.where(causal, s_new, -jnp.inf)
    p = jax.nn.softmax(jnp.concatenate([s_past, s_new], axis=-1), axis=-1)
    o = (jnp.einsum('bhqk,bkhd->bqhd', p[..., :past], v_past)
         + jnp.einsum('bhqk,bkhd->bqhd', p[..., past:], v_new.astype(jnp.float32)))
    return o.reshape(b, t, h * dh).astype(q.dtype)


def trunk_layer(x, c, conv_hist, ffn_hist, attend, w_ada, b_ada, g_pre_mix, g_post_mix, g_pre_ffn, g_post_ffn,
                w_in, b_f, w_dw, b_dw, ln_g, ln_b, w_out, w_up, w_ffn_dw, b_ffn_dw, w_down):
    b, t, _ = x.shape
    mod = (jax.nn.silu(c) @ w_ada + b_ada).reshape(b, N_MOD, 1, D_MODEL)
    shift1, scale1, gate1, shift2, scale2, gate2 = [mod[:, i] for i in range(N_MOD)]

    h = rms_norm(x, g_pre_mix) * (1 + scale1) + shift1
    p = h @ w_in
    q, k, v, f, u = jnp.split(p, [D_ATT, 2 * D_ATT, 3 * D_ATT, 3 * D_ATT + N_HEADS], axis=-1)
    q = q.reshape(b, t, N_HEADS, HEAD_DIM)
    k = k.reshape(b, t, N_HEADS, HEAD_DIM)
    v = v.reshape(b, t, N_HEADS, HEAD_DIM)
    logf = jax.nn.log_sigmoid((f + b_f).astype(jnp.float32))
    o_att = attend(q, k, v, logf)

    a, g = jnp.split(u, 2, axis=-1)
    z = a * jax.nn.sigmoid(g)
    z_ext = jnp.concatenate([conv_hist, z], axis=1)
    o_conv = jax.nn.silu(layer_norm(causal_dwconv(z_ext, w_dw, b_dw), ln_g, ln_b))
    conv_state = z_ext[:, -(CONV_WIDTH - 1):]

    mix = jnp.concatenate([o_att, o_conv], axis=-1) @ w_out
    x = x + gate1 * rms_norm(mix, g_post_mix)

    h2 = rms_norm(x, g_pre_ffn) * (1 + scale2) + shift2
    up = h2 @ w_up
    up_ext = jnp.concatenate([ffn_hist, up], axis=1)
    ua, ub = jnp.split(causal_dwconv(up_ext, w_ffn_dw, b_ffn_dw), 2, axis=-1)
    ffn = (jax.nn.silu(ua) * ub) @ w_down
    ffn_state = up_ext[:, -(FFN_CONV_WIDTH - 1):]
    x = x + gate2 * rms_norm(ffn, g_post_ffn)
    return x, k, v, logf.astype(x.dtype), conv_state, ffn_state


def setup_inputs(seed: int = 0) -> dict:
    key = jax.random.key(seed)
    ks = jax.random.split(key, 32)
    n_pages = PAST_LEN // PAGE_SIZE
    n_used = DEC_BATCH * n_pages
    n_phys = (n_used * 5) // 4
    nrm = lambda k, s, sc: jax.random.normal(k, s, jnp.float32) * sc
    page_table = jax.random.permutation(ks[0], n_phys)[:n_used].reshape(DEC_BATCH, n_pages).astype(jnp.int32)
    return {
        'x_prompt': nrm(ks[1], (BATCH, SEQ, D_MODEL), 1.0),
        'x_sample': nrm(ks[2], (DEC_BATCH, DEC_SEQ, D_MODEL), 1.0),
        'cache_k': nrm(ks[3], (DEPTH, n_phys, PAGE_SIZE, N_HEADS, HEAD_DIM), 1.0),
        'cache_v': nrm(ks[4], (DEPTH, n_phys, PAGE_SIZE, N_HEADS, HEAD_DIM), 1.0),
        'cache_logf': jax.nn.log_sigmoid(3.0 + nrm(ks[5], (DEPTH, n_phys, PAGE_SIZE, N_HEADS), 0.5)),
        'state_conv': nrm(ks[6], (DEPTH, DEC_BATCH, CONV_WIDTH - 1, D_CONV), 0.5),
        'state_ffn': nrm(ks[7], (DEPTH, DEC_BATCH, FFN_CONV_WIDTH - 1, 2 * D_FF), 1.0),
        'page_table': page_table,
        'c_prompt': nrm(ks[8], (BATCH, D_MODEL), 1.0),
        'c_sample': nrm(ks[9], (DEC_BATCH, D_MODEL), 1.0),
        'w_ada': nrm(ks[10], (DEPTH, D_MODEL, N_MOD * D_MODEL), 0.5 * D_MODEL ** -0.5),
        'b_ada': nrm(ks[11], (DEPTH, N_MOD * D_MODEL), 0.02),
        'g_pre_mix': 1.0 + nrm(ks[12], (DEPTH, D_MODEL), 0.05),
        'g_post_mix': 1.0 + nrm(ks[13], (DEPTH, D_MODEL), 0.05),
        'g_pre_ffn': 1.0 + nrm(ks[14], (DEPTH, D_MODEL), 0.05),
        'g_post_ffn': 1.0 + nrm(ks[15], (DEPTH, D_MODEL), 0.05),
        'w_in': nrm(ks[16], (DEPTH, D_MODEL, D_IN), D_MODEL ** -0.5),
        'b_f': 3.0 + nrm(ks[17], (DEPTH, N_HEADS), 0.5),
        'w_dw': nrm(ks[18], (DEPTH, CONV_WIDTH, D_CONV), CONV_WIDTH ** -0.5),
        'b_dw': nrm(ks[19], (DEPTH, D_CONV), 0.02),
        'ln_g': 1.0 + nrm(ks[20], (DEPTH, D_CONV), 0.05),
        'ln_b': nrm(ks[21], (DEPTH, D_CONV), 0.02),
        'w_out': nrm(ks[22], (DEPTH, D_MIX, D_MODEL), D_MIX ** -0.5),
        'w_up': nrm(ks[23], (DEPTH, D_MODEL, 2 * D_FF), D_MODEL ** -0.5),
        'w_ffn_dw': nrm(ks[24], (DEPTH, FFN_CONV_WIDTH, 2 * D_FF), FFN_CONV_WIDTH ** -0.5),
        'b_ffn_dw': nrm(ks[25], (DEPTH, 2 * D_FF), 0.02),
        'w_down': nrm(ks[26], (DEPTH, D_FF, D_MODEL), D_FF ** -0.5),
    }


def reference(x_prompt, x_sample, cache_k, cache_v, cache_logf, state_conv, state_ffn, page_table, c_prompt, c_sample,
              w_ada, b_ada, g_pre_mix, g_post_mix, g_pre_ffn, g_post_ffn, w_in, b_f, w_dw, b_dw, ln_g, ln_b,
              w_out, w_up, w_ffn_dw, b_ffn_dw, w_down):
    yp, ys = x_prompt, x_sample
    kp_l, vp_l, lfp_l, cp_l, fp_l = [], [], [], [], []
    ks_l, vs_l, lfs_l, cs_l, fs_l = [], [], [], [], []
    b = x_prompt.shape[0]
    for l in range(DEPTH):
        lw = (w_ada[l], b_ada[l], g_pre_mix[l], g_post_mix[l], g_pre_ffn[l], g_post_ffn[l], w_in[l], b_f[l],
              w_dw[l], b_dw[l], ln_g[l], ln_b[l], w_out[l], w_up[l], w_ffn_dw[l], b_ffn_dw[l], w_down[l])
        conv0 = jnp.zeros((b, CONV_WIDTH - 1, D_CONV), yp.dtype)
        ffn0 = jnp.zeros((b, FFN_CONV_WIDTH - 1, 2 * D_FF), yp.dtype)
        yp, kp, vp, lfp, cp, fp = trunk_layer(yp, c_prompt, conv0, ffn0, fox_prompt, *lw)
        attend_s = functools.partial(fox_sample, cache_k=cache_k[l], cache_v=cache_v[l],
                                     cache_logf=cache_logf[l], page_table=page_table)
        ys, kn, vn, lfn, cn, fn = trunk_layer(ys, c_sample, state_conv[l], state_ffn[l], attend_s, *lw)
        kp_l.append(kp); vp_l.append(vp); lfp_l.append(lfp); cp_l.append(cp); fp_l.append(fp)
        ks_l.append(kn); vs_l.append(vn); lfs_l.append(lfn); cs_l.append(cn); fs_l.append(fn)
    return (yp, ys,
            jnp.stack(kp_l), jnp.stack(vp_l), jnp.stack(lfp_l), jnp.stack(cp_l), jnp.stack(fp_l),
            jnp.stack(ks_l), jnp.stack(vs_l), jnp.stack(lfs_l), jnp.stack(cs_l), jnp.stack(fs_l))
```

```python
import functools

import jax
import jax.numpy as jnp
from jax import lax
from jax.experimental import pallas as pl
from jax.experimental.pallas import tpu as pltpu

F32 = jnp.float32
BF16 = jnp.bfloat16

D_MODEL = 1024
HEAD_DIM = 64
N_HEADS = 8
D_ATT = N_HEADS * HEAD_DIM
D_CONV = D_MODEL - D_ATT
CONV_WIDTH = 31
D_FF = 2816
FFN_CONV_WIDTH = 3
N_MOD = 6
EPS = 1e-6
PAGE_SIZE = 128

LANES = 128
SUBLANES = 8
FFN_CHUNK = 256
N_FFN_CHUNKS = D_FF // FFN_CHUNK
D_IN_PAD = 3 * D_ATT + 2 * D_CONV + LANES
VMEM_LIMIT = 52 * 1024 * 1024


def _sigmoid(x):
    return 1.0 / (1.0 + jnp.exp(-x))


def _rms(x):
    return x * lax.rsqrt(jnp.mean(x * x, axis=-1, keepdims=True) + EPS)


def _params(n_grid):
    return pltpu.CompilerParams(dimension_semantics=("arbitrary",) * n_grid,
                                vmem_limit_bytes=VMEM_LIMIT)


def _const_spec(shape):
    nd = len(shape)
    return pl.BlockSpec(shape, lambda *_: (0,) * nd, pipeline_mode=pl.Buffered(1))


def _ada_kernel(c_ref, w_ref, b_ref, o_ref):
    c = c_ref[...]
    s = (c * _sigmoid(c)).astype(BF16)
    o_ref[...] = jnp.dot(s, w_ref[...].astype(BF16), preferred_element_type=F32) + b_ref[...]


def _ada(c_all, w_ada, b_ada):
    r = c_all.shape[0]
    n = w_ada.shape[1]
    bn = D_MODEL
    return pl.pallas_call(
        _ada_kernel,
        grid=(n // bn,),
        in_specs=[pl.BlockSpec((r, D_MODEL), lambda j: (0, 0)),
                  pl.BlockSpec((D_MODEL, bn), lambda j: (0, j)),
                  pl.BlockSpec((1, bn), lambda j: (0, j))],
        out_specs=pl.BlockSpec((r, bn), lambda j: (0, j)),
        out_shape=jax.ShapeDtypeStruct((r, n), F32),
        compiler_params=_params(1),
        name="ada",
    )(c_all, w_ada, b_ada.reshape(1, n))


def _mix_in_kernel(x_ref, mod_ref, g_ref, w_ref, bf_ref,
                   q_ref, kb_ref, vb_ref, k_ref, v_ref, lf_ref, z_ref):
    x = x_ref[...]
    bs, bt, d = x.shape
    h = _rms(x) * g_ref[...] * (1.0 + mod_ref[:, 1:2, :]) + mod_ref[:, 0:1, :]
    h = h.reshape(bs * bt, d).astype(BF16)
    p = jnp.dot(h, w_ref[...], preferred_element_type=F32)
    q = p[:, 0:D_ATT]
    k = p[:, D_ATT:2 * D_ATT]
    v = p[:, 2 * D_ATT:3 * D_ATT]
    a = p[:, 3 * D_ATT:3 * D_ATT + D_CONV]
    g = p[:, 3 * D_ATT + D_CONV:3 * D_ATT + 2 * D_CONV]
    f = p[:, 3 * D_ATT + 2 * D_CONV:] + bf_ref[...]
    q_ref[...] = (q * (HEAD_DIM ** -0.5)).astype(BF16)
    kb_ref[...] = k.astype(BF16)
    vb_ref[...] = v.astype(BF16)
    k_ref[...] = k
    v_ref[...] = v
    lf = jnp.minimum(f, 0.0) - jnp.log(1.0 + jnp.exp(-jnp.abs(f)))
    lf_ref[...] = lf[:, 0:N_HEADS]
    z_ref[...] = a * _sigmoid(g)


def _mix_in(x3, mod, g_pre, w_in_r, bf_pad, bs, bt):
    n_seq, t, d = x3.shape
    n_tok = n_seq * t
    m = bs * bt
    nt = t // bt
    row = lambda i, j: (i * nt + j, 0)
    tok = lambda c, dt: jax.ShapeDtypeStruct((n_tok, c), dt)
    return pl.pallas_call(
        _mix_in_kernel,
        grid=(n_seq // bs, nt),
        in_specs=[pl.BlockSpec((bs, bt, d), lambda i, j: (i, j, 0)),
                  pl.BlockSpec((bs, N_MOD, d), lambda i, j: (i, 0, 0)),
                  _const_spec((1, d)),
                  _const_spec((d, D_IN_PAD)),
                  _const_spec((1, LANES))],
        out_specs=[pl.BlockSpec((m, D_ATT), row)] * 5
                  + [pl.BlockSpec((m, N_HEADS), row), pl.BlockSpec((m, D_CONV), row)],
        out_shape=[tok(D_ATT, BF16), tok(D_ATT, BF16), tok(D_ATT, BF16),
                   tok(D_ATT, F32), tok(D_ATT, F32), tok(N_HEADS, F32), tok(D_CONV, F32)],
        compiler_params=_params(2),
        name="mix_in",
    )(x3, mod, g_pre, w_in_r, bf_pad)


def _cumsum_lanes_kernel(x_ref, o_ref):
    x = x_ref[...]
    t = x.shape[-1]
    idx = lax.broadcasted_iota(jnp.int32, x.shape, 1)
    step = 1
    while step < t:
        x = x + jnp.where(idx >= step, pltpu.roll(x, step, axis=1), 0.0)
        step *= 2
    o_ref[...] = x


def _cumsum_lanes(x):
    b, h, t = x.shape
    return pl.pallas_call(
        _cumsum_lanes_kernel,
        grid=(b,),
        in_specs=[pl.BlockSpec((None, h, t), lambda i: (i, 0, 0))],
        out_specs=pl.BlockSpec((None, h, t), lambda i: (i, 0, 0)),
        out_shape=jax.ShapeDtypeStruct((b, h, t), F32),
        compiler_params=_params(1),
        name="cumsum_lanes",
    )(x)


def _cumsum_rows_kernel(x_ref, o_ref):
    x = x_ref[...]
    acc = x[:, 0:1, :]
    rows = [acc]
    for t in range(1, x.shape[1]):
        acc = acc + x[:, t:t + 1, :]
        rows.append(acc)
    o_ref[...] = jnp.concatenate(rows, axis=1)


def _cumsum_rows(x):
    return pl.pallas_call(
        _cumsum_rows_kernel,
        out_shape=jax.ShapeDtypeStruct(x.shape, F32),
        name="cumsum_rows",
    )(x)


def _fox_prompt_kernel(q_ref, k_ref, v_ref, cs_ref, ci_ref, o_ref, *, tq):
    qi = pl.program_id(2)
    q = q_ref[...]
    lane = lax.broadcasted_iota(jnp.int32, (1, LANES), 1)
    row = lax.broadcasted_iota(jnp.int32, (tq, tq), 0)
    col = lax.broadcasted_iota(jnp.int32, (tq, tq), 1)
    res = []
    for h in range(2):
        hm = (lane // HEAD_DIM) == h
        qh = jnp.where(hm, q, jnp.zeros_like(q))
        ci = ci_ref[:, h:h + 1]

        def block(kb, carry, masked, hm=hm, qh=qh, ci=ci, h=h):
            m, acc = carry
            off = pl.multiple_of(kb * tq, tq)
            kblk = k_ref[pl.ds(off, tq), :]
            vblk = jnp.where(hm, v_ref[pl.ds(off, tq), :], jnp.ones((), BF16))
            s = lax.dot_general(qh, kblk, (((1,), (1,)), ((), ())), preferred_element_type=F32)
            u = s - cs_ref[h:h + 1, pl.ds(off, tq)]
            if masked:
                u = jnp.where(col <= row, u, -jnp.inf)
            m_new = jnp.maximum(m, jnp.max(u, axis=1, keepdims=True) + ci)
            p = jnp.exp(u - (m_new - ci))
            acc = jnp.exp(m - m_new) * acc + jnp.dot(p.astype(BF16), vblk, preferred_element_type=F32)
            return m_new, acc

        init = (jnp.full((tq, 1), -jnp.inf, F32), jnp.zeros((tq, LANES), F32))
        carry = lax.fori_loop(0, qi, functools.partial(block, masked=False), init)
        _, acc = block(qi, carry, True)
        res.append(acc / pltpu.roll(acc, HEAD_DIM, axis=1))
    o_ref[...] = jnp.where((lane // HEAD_DIM) == 0, res[0], res[1]).astype(o_ref.dtype)


def _fox_prompt(q, kb, vb, cum_row, cum_col, b, t, tq):
    nq = t // tq
    n_pairs = N_HEADS // 2
    return pl.pallas_call(
        functools.partial(_fox_prompt_kernel, tq=tq),
        grid=(b, n_pairs, nq),
        in_specs=[pl.BlockSpec((tq, LANES), lambda i, p, j: (i * nq + j, p)),
                  pl.BlockSpec((t, LANES), lambda i, p, j: (i, p)),
                  pl.BlockSpec((t, LANES), lambda i, p, j: (i, p)),
                  pl.BlockSpec((None, None, 2, t), lambda i, p, j: (i, p, 0, 0)),
                  pl.BlockSpec((None, None, tq, 2), lambda i, p, j: (i, p, j, 0))],
        out_specs=pl.BlockSpec((tq, LANES), lambda i, p, j: (i * nq + j, p)),
        out_shape=jax.ShapeDtypeStruct((b * t, D_ATT), BF16),
        compiler_params=_params(3),
        name="fox_prompt",
    )(q, kb, vb, cum_row, cum_col)


def _fox_sample_kernel(pt_ref, q_ref, kn_ref, vn_ref, cnr_ref, cnc_ref, *rest, n_grp):
    del pt_ref
    k_pages = rest[0:n_grp]
    v_pages = rest[n_grp:2 * n_grp]
    lf_pages = rest[2 * n_grp:3 * n_grp]
    o_ref = rest[3 * n_grp]
    qbd_ref, m_ref, l_ref, acc_ref, carry_ref = rest[3 * n_grp + 1:]
    j = pl.program_id(1)
    t_new = q_ref.shape[0]
    n_rows = t_new * N_HEADS
    head_of_lane = lax.broadcasted_iota(jnp.int32, (N_HEADS, D_ATT), 1) // HEAD_DIM
    head_mask = (head_of_lane == lax.broadcasted_iota(jnp.int32, (N_HEADS, D_ATT), 0)).astype(F32)

    @pl.when(j == 0)
    def _():
        q = q_ref[...].astype(F32)
        qbd = (q[:, None, :] * head_mask[None]).reshape(n_rows, D_ATT)
        qbd_ref[...] = qbd.astype(BF16)
        s = lax.dot_general(qbd, kn_ref[...], (((1,), (1,)), ((), ())),
                            preferred_element_type=F32)
        u = (s.reshape(t_new, N_HEADS, t_new) - cnc_ref[...][None]).reshape(n_rows, t_new)
        u = u + cnr_ref[...]
        tok = lax.broadcasted_iota(jnp.int32, (n_rows, t_new), 0) // N_HEADS
        key = lax.broadcasted_iota(jnp.int32, (n_rows, t_new), 1)
        u = jnp.where(key <= tok, u, -jnp.inf)
        m = jnp.max(u, axis=1, keepdims=True)
        p = jnp.exp(u - m)
        m_ref[...] = m
        l_ref[...] = jnp.sum(p, axis=1, keepdims=True)
        acc_ref[...] = jnp.dot(p, vn_ref[...], preferred_element_type=F32)
        carry_ref[...] = jnp.zeros_like(carry_ref)

    later = (lax.broadcasted_iota(jnp.int32, (PAGE_SIZE, PAGE_SIZE), 0)
             > lax.broadcasted_iota(jnp.int32, (PAGE_SIZE, PAGE_SIZE), 1)).astype(F32)
    carry = carry_ref[...]
    rests = [None] * n_grp
    for g in reversed(range(n_grp)):
        lf = lf_pages[g][...]
        rests[g] = jnp.dot(lf, later, precision=lax.Precision.HIGHEST,
                           preferred_element_type=F32) + carry
        carry = carry + jnp.sum(lf, axis=1, keepdims=True)
    carry_ref[...] = carry
    rest_all = jnp.concatenate(rests, axis=1)

    qbd = qbd_ref[...]
    s = jnp.concatenate(
        [lax.dot_general(qbd, k_pages[g][...].astype(BF16), (((1,), (1,)), ((), ())),
                         preferred_element_type=F32) for g in range(n_grp)], axis=1)
    n_keys = n_grp * PAGE_SIZE
    u = (s.reshape(t_new, N_HEADS, n_keys) + rest_all[None]).reshape(n_rows, n_keys)
    u = u + cnr_ref[...]
    m_old = m_ref[...]
    m_new = jnp.maximum(m_old, jnp.max(u, axis=1, keepdims=True))
    alpha = jnp.exp(m_old - m_new)
    p = jnp.exp(u - m_new)
    m_ref[...] = m_new
    l_ref[...] = alpha * l_ref[...] + jnp.sum(p, axis=1, keepdims=True)
    pb = p.astype(BF16)
    pv = jnp.dot(pb[:, 0:PAGE_SIZE], v_pages[0][...].astype(BF16), preferred_element_type=F32)
    for g in range(1, n_grp):
        pv = pv + jnp.dot(pb[:, g * PAGE_SIZE:(g + 1) * PAGE_SIZE], v_pages[g][...].astype(BF16),
                          preferred_element_type=F32)
    acc_ref[...] = alpha * acc_ref[...] + pv

    @pl.when(j == pl.num_programs(1) - 1)
    def _():
        o = acc_ref[...] / l_ref[...]
        o = jnp.sum(o.reshape(t_new, N_HEADS, D_ATT) * head_mask[None], axis=1)
        o_ref[...] = o.astype(o_ref.dtype)


def _fox_sample(q3, kn3, vn3, cn_rows, cn_cols, cache_k, cache_v, cache_lft, page_table, n_grp):
    s_, t_new, _ = q3.shape
    n_pages = page_table.shape[1]
    n_steps = n_pages // n_grp
    n_rows = t_new * N_HEADS

    def page_map(g):
        return lambda s, j, pt: (pt[s * n_pages + (n_steps - 1 - j) * n_grp + g], 0, 0)

    seq3 = lambda s, j, pt: (s, 0, 0)
    in_specs = [pl.BlockSpec((None, t_new, D_ATT), seq3),
                pl.BlockSpec((None, t_new, D_ATT), seq3),
                pl.BlockSpec((None, t_new, D_ATT), seq3),
                pl.BlockSpec((None, n_rows, 1), seq3),
                pl.BlockSpec((None, N_HEADS, t_new), seq3)]
    in_specs += [pl.BlockSpec((None, PAGE_SIZE, D_ATT), page_map(g)) for g in range(n_grp)]
    in_specs += [pl.BlockSpec((None, PAGE_SIZE, D_ATT), page_map(g)) for g in range(n_grp)]
    in_specs += [pl.BlockSpec((None, N_HEADS, PAGE_SIZE), page_map(g)) for g in range(n_grp)]
    grid_spec = pltpu.PrefetchScalarGridSpec(
        num_scalar_prefetch=1,
        grid=(s_, n_steps),
        in_specs=in_specs,
        out_specs=pl.BlockSpec((None, t_new, D_ATT), seq3),
        scratch_shapes=[pltpu.VMEM((n_rows, D_ATT), BF16),
                        pltpu.VMEM((n_rows, 1), F32),
                        pltpu.VMEM((n_rows, 1), F32),
                        pltpu.VMEM((n_rows, D_ATT), F32),
                        pltpu.VMEM((N_HEADS, 1), F32)])
    return pl.pallas_call(
        functools.partial(_fox_sample_kernel, n_grp=n_grp),
        grid_spec=grid_spec,
        out_shape=jax.ShapeDtypeStruct((s_, t_new, D_ATT), BF16),
        compiler_params=_params(2),
        name="fox_sample",
    )(page_table.reshape(-1), q3, kn3, vn3, cn_rows, cn_cols,
      *([cache_k] * n_grp), *([cache_v] * n_grp), *([cache_lft] * n_grp))


HALO = 32


def _conv_kernel(z_ref, hist_ref, w_ref, b_ref, g_ref, bb_ref, o_ref, st_ref, ext_ref, *, bt):
    t = pl.program_id(1)
    pad = HALO - (CONV_WIDTH - 1)

    @pl.when(t == 0)
    def _():
        ext_ref[0:pad, :] = jnp.zeros((pad, D_CONV), F32)
        ext_ref[pad:HALO, :] = hist_ref[...]

    @pl.when(t > 0)
    def _():
        ext_ref[0:HALO, :] = ext_ref[bt:bt + HALO, :]

    ext_ref[HALO:HALO + bt, :] = z_ref[...]
    acc = jnp.zeros((bt, D_CONV), F32) + b_ref[...]
    for j in range(CONV_WIDTH):
        acc = acc + w_ref[j:j + 1, :] * ext_ref[pl.ds(j + pad, bt), :]
    mu = jnp.mean(acc, axis=-1, keepdims=True)
    xc = acc - mu
    var = jnp.mean(xc * xc, axis=-1, keepdims=True)
    y = xc * lax.rsqrt(var + EPS) * g_ref[...] + bb_ref[...]
    o_ref[...] = (y * _sigmoid(y)).astype(o_ref.dtype)

    @pl.when(t == pl.num_programs(1) - 1)
    def _():
        st_ref[...] = ext_ref[bt + pad:bt + HALO, :]


def _conv(z3, hist, w_dw, b_dw, ln_g, ln_b, bt):
    n_seq, t, c = z3.shape
    k1 = CONV_WIDTH - 1
    return pl.pallas_call(
        functools.partial(_conv_kernel, bt=bt),
        grid=(n_seq, t // bt),
        in_specs=[pl.BlockSpec((None, bt, c), lambda i, j: (i, j, 0)),
                  pl.BlockSpec((None, k1, c), lambda i, j: (i, 0, 0)),
                  _const_spec((CONV_WIDTH, c)),
                  _const_spec((1, c)), _const_spec((1, c)), _const_spec((1, c))],
        out_specs=[pl.BlockSpec((None, bt, c), lambda i, j: (i, j, 0)),
                   pl.BlockSpec((None, k1, c), lambda i, j: (i, 0, 0))],
        out_shape=[jax.ShapeDtypeStruct((n_seq, t, c), BF16),
                   jax.ShapeDtypeStruct((n_seq, k1, c), F32)],
        scratch_shapes=[pltpu.VMEM((HALO + bt, c), F32)],
        compiler_params=_params(2),
        name="conv",
    )(z3, hist, w_dw, b_dw.reshape(1, c), ln_g.reshape(1, c), ln_b.reshape(1, c))


def _post_ffn_kernel(x_ref, oa_ref, oc_ref, mod_ref, hist_ref, wo_ref, gpm_ref, gpf_ref, gqf_ref,
                     wu_ref, cw_ref, cb_ref, wd_ref, y_ref, st_ref,
                     carry_ref, ext_ref, acc_ref, h2_ref):
    t = pl.program_id(1)
    bs, bt, d = x_ref.shape
    m = bs * bt
    ck = FFN_CHUNK
    k1 = FFN_CONV_WIDTH - 1

    mix = (jnp.dot(oa_ref[...], wo_ref[0:D_ATT, :], preferred_element_type=F32)
           + jnp.dot(oc_ref[...], wo_ref[D_ATT:, :], preferred_element_type=F32))
    x1 = x_ref[...] + mod_ref[:, 2:3, :] * (_rms(mix.reshape(bs, bt, d)) * gpm_ref[...])
    y_ref[...] = x1
    h2 = _rms(x1) * gpf_ref[...] * (1.0 + mod_ref[:, 4:5, :]) + mod_ref[:, 3:4, :]
    h2_ref[...] = h2.reshape(m, d).astype(BF16)
    acc_ref[...] = jnp.zeros_like(acc_ref)

    @pl.when(t == 0)
    def _():
        carry_ref[...] = jnp.zeros_like(carry_ref)
        carry_ref[:, :, SUBLANES - k1:SUBLANES, :] = hist_ref[...]

    def conv_half(idx):
        up = jnp.dot(h2_ref[...], wu_ref[idx], preferred_element_type=F32).reshape(bs, bt, ck)
        ext_ref[:, 0:SUBLANES, :] = carry_ref[idx]
        ext_ref[:, SUBLANES:, :] = up
        carry_ref[idx] = up[:, bt - SUBLANES:, :]
        st_ref[idx] = up[:, bt - k1:, :]
        w = cw_ref[idx]
        return (w[0:1, :] * ext_ref[:, pl.ds(SUBLANES - 2, bt), :]
                + w[1:2, :] * ext_ref[:, pl.ds(SUBLANES - 1, bt), :]
                + w[2:3, :] * up + cb_ref[idx])

    def chunk(c, _):
        ua = conv_half(c)
        ub = conv_half(c + N_FFN_CHUNKS)
        g = (ua * _sigmoid(ua) * ub).reshape(m, ck).astype(BF16)
        acc_ref[...] += jnp.dot(g, wd_ref[c], preferred_element_type=F32)
        return 0

    lax.fori_loop(0, N_FFN_CHUNKS, chunk, 0)
    ffn = acc_ref[...].reshape(bs, bt, d)
    y_ref[...] = y_ref[...] + mod_ref[:, 5:6, :] * (_rms(ffn) * gqf_ref[...])


def _post_ffn(x3, o_att, o_conv, mod, hist_c, w_out, g_post_mix, g_pre_ffn, g_post_ffn,
              w_up_c, cw_c, cb_c, w_down_c, bs, bt):
    n_seq, t, d = x3.shape
    m = bs * bt
    nt = t // bt
    k1 = FFN_CONV_WIDTH - 1
    nc2 = 2 * N_FFN_CHUNKS
    row = lambda i, j: (i * nt + j, 0)
    return pl.pallas_call(
        _post_ffn_kernel,
        grid=(n_seq // bs, nt),
        in_specs=[pl.BlockSpec((bs, bt, d), lambda i, j: (i, j, 0)),
                  pl.BlockSpec((m, D_ATT), row),
                  pl.BlockSpec((m, D_CONV), row),
                  pl.BlockSpec((bs, N_MOD, d), lambda i, j: (i, 0, 0)),
                  pl.BlockSpec((nc2, bs, k1, FFN_CHUNK), lambda i, j: (0, i, 0, 0)),
                  _const_spec((d, d)),
                  _const_spec((1, d)), _const_spec((1, d)), _const_spec((1, d)),
                  _const_spec((nc2, d, FFN_CHUNK)),
                  _const_spec((nc2, FFN_CONV_WIDTH, FFN_CHUNK)),
                  _const_spec((nc2, 1, FFN_CHUNK)),
                  _const_spec((N_FFN_CHUNKS, FFN_CHUNK, d))],
        out_specs=[pl.BlockSpec((bs, bt, d), lambda i, j: (i, j, 0)),
                   pl.BlockSpec((nc2, bs, k1, FFN_CHUNK), lambda i, j: (0, i, 0, 0))],
        out_shape=[jax.ShapeDtypeStruct((n_seq, t, d), F32),
                   jax.ShapeDtypeStruct((nc2, n_seq, k1, FFN_CHUNK), F32)],
        scratch_shapes=[pltpu.VMEM((nc2, bs, SUBLANES, FFN_CHUNK), F32),
                        pltpu.VMEM((bs, SUBLANES + bt, FFN_CHUNK), F32),
                        pltpu.VMEM((m, d), F32),
                        pltpu.VMEM((m, d), BF16)],
        compiler_params=_params(2),
        name="post_ffn",
    )(x3, o_att, o_conv, mod, hist_c, w_out, g_post_mix, g_pre_ffn, g_post_ffn,
      w_up_c, cw_c, cb_c, w_down_c)


def _chunk_cols(a):
    lead = a.shape[:-1]
    a = a.reshape(lead + (2 * N_FFN_CHUNKS, FFN_CHUNK))
    return jnp.moveaxis(a, -2, 0)


def _unchunk_cols(a):
    a = jnp.moveaxis(a, 0, -2)
    return a.reshape(a.shape[:-2] + (2 * D_FF,))


PROMPT_TILES = dict(mix=(1, 512), conv=256, ffn=(1, 256), tq=256)
SAMPLE_TILES = dict(mix=(32, 8), conv=8, ffn=(16, 8), n_grp=8)


def _layer(x3, mod, conv_hist, ffn_hist, attend, tiles, lw):
    n_seq, t, d = x3.shape
    q, kb, vb, k, v, lf, z = _mix_in(x3, mod, lw["g_pre_mix"], lw["w_in_r"], lw["bf_pad"], *tiles["mix"])
    o_att = attend(q, kb, vb, k, v, lf)
    o_conv, conv_state = _conv(z.reshape(n_seq, t, D_CONV), conv_hist, lw["w_dw"], lw["b_dw"],
                               lw["ln_g"], lw["ln_b"], tiles["conv"])
    y, ffn_state_c = _post_ffn(x3, o_att, o_conv.reshape(n_seq * t, D_CONV), mod, _chunk_cols(ffn_hist),
                               lw["w_out"], lw["g_post_mix"], lw["g_pre_ffn"], lw["g_post_ffn"],
                               lw["w_up_c"], lw["cw_c"], lw["cb_c"], lw["w_down_c"], *tiles["ffn"])
    return (y, k.reshape(n_seq, t, N_HEADS, HEAD_DIM), v.reshape(n_seq, t, N_HEADS, HEAD_DIM),
            lf.reshape(n_seq, t, N_HEADS), conv_state, _unchunk_cols(ffn_state_c))


def kernel(x_prompt, x_sample, cache_k, cache_v, cache_logf, state_conv, state_ffn, page_table, c_prompt, c_sample, w_ada, b_ada, g_pre_mix, g_post_mix, g_pre_ffn, g_post_ffn, w_in, b_f, w_dw, b_dw, ln_g, ln_b, w_out, w_up, w_ffn_dw, b_ffn_dw, w_down):
    depth = w_ada.shape[0]
    b, t, d = x_prompt.shape
    s_, t_new, _ = x_sample.shape
    n_phys = cache_k.shape[1]
    yp, ys = x_prompt, x_sample
    outs_p, outs_s = [], []
    c_all = jnp.concatenate([c_prompt, c_sample], axis=0)
    c_all = jnp.pad(c_all, ((0, (-c_all.shape[0]) % SUBLANES), (0, 0)))
    for l in range(depth):
        row = lambda a: a[l].reshape(1, -1)
        wi = w_in[l]
        w_in_r = jnp.concatenate(
            [wi[:, :3 * D_ATT], wi[:, 3 * D_ATT + N_HEADS:],
             jnp.pad(wi[:, 3 * D_ATT:3 * D_ATT + N_HEADS], ((0, 0), (0, LANES - N_HEADS)))],
            axis=1).astype(BF16)
        lw = dict(
            g_pre_mix=row(g_pre_mix), g_post_mix=row(g_post_mix),
            g_pre_ffn=row(g_pre_ffn), g_post_ffn=row(g_post_ffn),
            w_in_r=w_in_r, bf_pad=jnp.pad(b_f[l], (0, LANES - N_HEADS)).reshape(1, LANES),
            w_dw=w_dw[l], b_dw=b_dw[l], ln_g=ln_g[l], ln_b=ln_b[l],
            w_out=w_out[l].astype(BF16),
            w_up_c=_chunk_cols(w_up[l].astype(BF16)),
            cw_c=_chunk_cols(w_ffn_dw[l]),
            cb_c=_chunk_cols(b_ffn_dw[l].reshape(1, -1)),
            w_down_c=w_down[l].astype(BF16).reshape(N_FFN_CHUNKS, FFN_CHUNK, d),
        )
        mod = _ada(c_all, w_ada[l], b_ada[l])
        mod_p = mod[:b].reshape(b, N_MOD, d)
        mod_s = mod[b:b + s_].reshape(s_, N_MOD, d)

        def attend_p(q, kb, vb, k, v, lf):
            cum = _cumsum_lanes(lf.reshape(b, t, N_HEADS).transpose(0, 2, 1))
            cum_row = cum.reshape(b, N_HEADS // 2, 2, t)
            cum_col = cum_row.transpose(0, 1, 3, 2)
            return _fox_prompt(q, kb, vb, cum_row, cum_col, b, t, PROMPT_TILES["tq"])

        def attend_s(q, kb, vb, k, v, lf):
            cn = _cumsum_rows(lf.reshape(s_, t_new, N_HEADS))
            o = _fox_sample(q.reshape(s_, t_new, D_ATT), k.reshape(s_, t_new, D_ATT),
                            v.reshape(s_, t_new, D_ATT),
                            cn.reshape(s_, t_new * N_HEADS, 1), cn.transpose(0, 2, 1),
                            cache_k[l].reshape(n_phys, PAGE_SIZE, D_ATT),
                            cache_v[l].reshape(n_phys, PAGE_SIZE, D_ATT),
                            cache_logf[l].transpose(0, 2, 1), page_table, SAMPLE_TILES["n_grp"])
            return o.reshape(s_ * t_new, D_ATT)

        conv0 = jnp.zeros((b, CONV_WIDTH - 1, D_CONV), F32)
        ffn0 = jnp.zeros((b, FFN_CONV_WIDTH - 1, 2 * D_FF), F32)
        yp, *rp = _layer(yp, mod_p, conv0, ffn0, attend_p, PROMPT_TILES, lw)
        ys, *rs = _layer(ys, mod_s, state_conv[l], state_ffn[l], attend_s, SAMPLE_TILES, lw)
        outs_p.append(rp)
        outs_s.append(rs)
    stack = lambda outs, i: jnp.stack([o[i] for o in outs])
    return (yp, ys, *[stack(outs_p, i) for i in range(5)], *[stack(outs_s, i) for i in range(5)])
```

```python
import functools

import jax
import jax.numpy as jnp
from jax import lax
from jax.experimental import pallas as pl
from jax.experimental.pallas import tpu as pltpu

F32 = jnp.float32
BF16 = jnp.bfloat16

D_MODEL = 1024
HEAD_DIM = 64
N_HEADS = 8
D_ATT = N_HEADS * HEAD_DIM
D_CONV = D_MODEL - D_ATT
CONV_WIDTH = 31
D_FF = 2816
FFN_CONV_WIDTH = 3
N_MOD = 6
EPS = 1e-6
PAGE_SIZE = 128

LANES = 128
SUBLANES = 8
FFN_CHUNK = 256
N_FFN_CHUNKS = D_FF // FFN_CHUNK
D_IN_PAD = 3 * D_ATT + 2 * D_CONV + LANES
VMEM_LIMIT = 52 * 1024 * 1024


def _sigmoid(x):
    return 1.0 / (1.0 + jnp.exp(-x))


def _rms(x):
    return x * lax.rsqrt(jnp.mean(x * x, axis=-1, keepdims=True) + EPS)


def _params(n_grid):
    return pltpu.CompilerParams(dimension_semantics=("arbitrary",) * n_grid,
                                vmem_limit_bytes=VMEM_LIMIT)


def _const_spec(shape):
    nd = len(shape)
    return pl.BlockSpec(shape, lambda *_: (0,) * nd, pipeline_mode=pl.Buffered(1))


def _ada_kernel(c_ref, w_ref, b_ref, o_ref):
    c = c_ref[...]
    s = (c * _sigmoid(c)).astype(BF16)
    o_ref[...] = jnp.dot(s, w_ref[...].astype(BF16), preferred_element_type=F32) + b_ref[...]


def _ada(c_all, w_ada, b_ada):
    r = c_all.shape[0]
    n = w_ada.shape[1]
    bn = D_MODEL
    return pl.pallas_call(
        _ada_kernel,
        grid=(n // bn,),
        in_specs=[pl.BlockSpec((r, D_MODEL), lambda j: (0, 0)),
                  pl.BlockSpec((D_MODEL, bn), lambda j: (0, j)),
                  pl.BlockSpec((1, bn), lambda j: (0, j))],
        out_specs=pl.BlockSpec((r, bn), lambda j: (0, j)),
        out_shape=jax.ShapeDtypeStruct((r, n), F32),
        compiler_params=_params(1),
        name="ada",
    )(c_all, w_ada, b_ada.reshape(1, n))


def _mix_in_kernel(x_ref, mod_ref, g_ref, w_ref, bf_ref,
                   q_ref, kb_ref, vb_ref, k_ref, v_ref, lf_ref, z_ref):
    x = x_ref[...]
    bs, bt, d = x.shape
    h = _rms(x) * g_ref[...] * (1.0 + mod_ref[:, 1:2, :]) + mod_ref[:, 0:1, :]
    h = h.reshape(bs * bt, d).astype(BF16)
    p = jnp.dot(h, w_ref[...], preferred_element_type=F32)
    q = p[:, 0:D_ATT]
    k = p[:, D_ATT:2 * D_ATT]
    v = p[:, 2 * D_ATT:3 * D_ATT]
    a = p[:, 3 * D_ATT:3 * D_ATT + D_CONV]
    g = p[:, 3 * D_ATT + D_CONV:3 * D_ATT + 2 * D_CONV]
    f = p[:, 3 * D_ATT + 2 * D_CONV:] + bf_ref[...]
    q_ref[...] = (q * (HEAD_DIM ** -0.5)).astype(BF16)
    kb_ref[...] = k.astype(BF16)
    vb_ref[...] = v.astype(BF16)
    k_ref[...] = k
    v_ref[...] = v
    lf = jnp.minimum(f, 0.0) - jnp.log(1.0 + jnp.exp(-jnp.abs(f)))
    lf_ref[...] = lf[:, 0:N_HEADS]
    z_ref[...] = a * _sigmoid(g)


def _mix_in(x3, mod, g_pre, w_in_r, bf_pad, bs, bt):
    n_seq, t, d = x3.shape
    n_tok = n_seq * t
    m = bs * bt
    nt = t // bt
    row = lambda i, j: (i * nt + j, 0)
    tok = lambda c, dt: jax.ShapeDtypeStruct((n_tok, c), dt)
    return pl.pallas_call(
        _mix_in_kernel,
        grid=(n_seq // bs, nt),
        in_specs=[pl.BlockSpec((bs, bt, d), lambda i, j: (i, j, 0)),
                  pl.BlockSpec((bs, N_MOD, d), lambda i, j: (i, 0, 0)),
                  _const_spec((1, d)),
                  _const_spec((d, D_IN_PAD)),
                  _const_spec((1, LANES))],
        out_specs=[pl.BlockSpec((m, D_ATT), row)] * 5
                  + [pl.BlockSpec((m, N_HEADS), row), pl.BlockSpec((m, D_CONV), row)],
        out_shape=[tok(D_ATT, BF16), tok(D_ATT, BF16), tok(D_ATT, BF16),
                   tok(D_ATT, F32), tok(D_ATT, F32), tok(N_HEADS, F32), tok(D_CONV, F32)],
        compiler_params=_params(2),
        name="mix_in",
    )(x3, mod, g_pre, w_in_r, bf_pad)


def _cumsum_lanes_kernel(x_ref, o_ref):
    x = x_ref[...]
    t = x.shape[-1]
    idx = lax.broadcasted_iota(jnp.int32, x.shape, 1)
    step = 1
    while step < t:
        x = x + jnp.where(idx >= step, pltpu.roll(x, step, axis=1), 0.0)
        step *= 2
    o_ref[...] = x


def _cumsum_lanes(x):
    b, h, t = x.shape
    return pl.pallas_call(
        _cumsum_lanes_kernel,
        grid=(b,),
        in_specs=[pl.BlockSpec((None, h, t), lambda i: (i, 0, 0))],
        out_specs=pl.BlockSpec((None, h, t), lambda i: (i, 0, 0)),
        out_shape=jax.ShapeDtypeStruct((b, h, t), F32),
        compiler_params=_params(1),
        name="cumsum_lanes",
    )(x)


def _cumsum_rows_kernel(x_ref, o_ref):
    x = x_ref[...]
    acc = x[:, 0:1, :]
    rows = [acc]
    for t in range(1, x.shape[1]):
        acc = acc + x[:, t:t + 1, :]
        rows.append(acc)
    o_ref[...] = jnp.concatenate(rows, axis=1)


def _cumsum_rows(x):
    return pl.pallas_call(
        _cumsum_rows_kernel,
        out_shape=jax.ShapeDtypeStruct(x.shape, F32),
        name="cumsum_rows",
    )(x)


def _split3(x):
    hi = x.astype(BF16).astype(F32)
    r = x - hi
    mid = r.astype(BF16).astype(F32)
    lo = (r - mid).astype(BF16).astype(F32)
    return hi, mid, lo


N_BIAS = 3


def _augment(x, own, lane, base, pieces, fill):
    out = jnp.where(own, x, 0.0)
    for i in range(N_BIAS):
        out = jnp.where(lane == base + i, pieces[i], out)
    return jnp.where((lane >= base + N_BIAS) & (lane < base + 2 * N_BIAS), fill, out)


def _fox_prompt_kernel(q_ref, k_ref, v_ref, cs_ref, ci_ref, o_ref,
                       kaug_ref, vt_ref, qt_ref, m_ref, acc_ref, *, tq):
    qi = pl.program_id(2)
    t = k_ref.shape[0]
    nk = t // tq
    lane = lax.broadcasted_iota(jnp.int32, (1, LANES), 1)
    sub_head = lax.broadcasted_iota(jnp.int32, (LANES, 1), 0) // HEAD_DIM

    @pl.when(qi == 0)
    def _():
        def prep(c, _):
            off = pl.multiple_of(c * tq, tq)
            kc = k_ref[pl.ds(off, tq), :].astype(F32)
            vt = v_ref[pl.ds(off, tq), :].astype(F32).T
            for h in range(2):
                own = (lane // HEAD_DIM) == h
                cs = jnp.broadcast_to(cs_ref[pl.ds(off, tq), h:h + 1], (tq, LANES))
                kaug = _augment(kc, own, lane, HEAD_DIM * (1 - h), _split3(cs), 1.0)
                kaug_ref[h, c] = kaug.astype(BF16)
                vt_ref[h, c] = jnp.where(sub_head == h, vt, 1.0).astype(BF16)
            return 0
        lax.fori_loop(0, nk, prep, 0)

    q = q_ref[...].astype(F32)
    for h in range(2):
        own = (lane // HEAD_DIM) == h
        ci = jnp.broadcast_to(ci_ref[:, h:h + 1], (tq, LANES))
        base = HEAD_DIM * (1 - h)
        qaug = _augment(q, own, lane, base + N_BIAS, _split3(ci), 0.0)
        qaug = jnp.where((lane >= base) & (lane < base + N_BIAS), -1.0, qaug)
        qt_ref[h] = qaug.T.astype(BF16)
    m_ref[...] = jnp.full(m_ref.shape, -jnp.inf, F32)
    acc_ref[...] = jnp.zeros_like(acc_ref)
    key_pos = lax.broadcasted_iota(jnp.int32, (tq, tq), 0)
    qry_pos = lax.broadcasted_iota(jnp.int32, (tq, tq), 1)

    def block(kb, masked):
        for h in range(2):
            ut = jnp.dot(kaug_ref[h, kb], qt_ref[h], preferred_element_type=F32)
            if masked:
                ut = jnp.where(key_pos <= qry_pos, ut, -jnp.inf)
            m_old = m_ref[h]
            m_new = jnp.maximum(m_old, jnp.max(ut, axis=0, keepdims=True))
            p = jnp.exp(ut - m_new).astype(BF16)
            m_ref[h] = m_new
            acc_ref[h] = (jnp.exp(m_old - m_new) * acc_ref[h]
                          + jnp.dot(vt_ref[h, kb], p, preferred_element_type=F32))

    def body(kb, _):
        block(kb, False)
        return 0

    lax.fori_loop(0, qi, body, 0)
    block(qi, True)
    a0 = acc_ref[0]
    a1 = acc_ref[1]
    ot = jnp.concatenate([a0[0:HEAD_DIM] / a0[HEAD_DIM:HEAD_DIM + 1],
                          a1[HEAD_DIM:] / a1[0:1]], axis=0)
    o_ref[...] = ot.T.astype(o_ref.dtype)


def _fox_prompt(q, kb, vb, cum_col, b, t, tq):
    nq = t // tq
    n_pairs = N_HEADS // 2
    return pl.pallas_call(
        functools.partial(_fox_prompt_kernel, tq=tq),
        grid=(b, n_pairs, nq),
        in_specs=[pl.BlockSpec((tq, LANES), lambda i, p, j: (i * nq + j, p)),
                  pl.BlockSpec((t, LANES), lambda i, p, j: (i, p)),
                  pl.BlockSpec((t, LANES), lambda i, p, j: (i, p)),
                  pl.BlockSpec((None, None, t, 2), lambda i, p, j: (i, p, 0, 0)),
                  pl.BlockSpec((None, None, tq, 2), lambda i, p, j: (i, p, j, 0))],
        out_specs=pl.BlockSpec((tq, LANES), lambda i, p, j: (i * nq + j, p)),
        out_shape=jax.ShapeDtypeStruct((b * t, D_ATT), BF16),
        scratch_shapes=[pltpu.VMEM((2, nq, tq, LANES), BF16),
                        pltpu.VMEM((2, nq, LANES, tq), BF16),
                        pltpu.VMEM((2, LANES, tq), BF16),
                        pltpu.VMEM((2, 1, tq), F32),
                        pltpu.VMEM((2, LANES, tq), F32)],
        compiler_params=_params(3),
        name="fox_prompt",
    )(q, kb, vb, cum_col, cum_col)


PAGE_ROWS = PAGE_SIZE * N_HEADS


def _fox_sample_kernel(pt_ref, q_ref, kn_ref, vn_ref, cnr_ref, cnc_ref, *rest, n_grp):
    del pt_ref
    k_pages = rest[0:n_grp]
    v_pages = rest[n_grp:2 * n_grp]
    lf_pages = rest[2 * n_grp:3 * n_grp]
    o_ref = rest[3 * n_grp]
    m_ref, l_ref, acc_ref, carry_ref = rest[3 * n_grp + 1:]
    j = pl.program_id(1)
    n_rows = q_ref.shape[0]
    row_i = lax.broadcasted_iota(jnp.int32, (n_rows, n_rows), 0)
    col_i = lax.broadcasted_iota(jnp.int32, (n_rows, n_rows), 1)

    @pl.when(j == 0)
    def _():
        s = lax.dot_general(q_ref[...].astype(F32), kn_ref[...], (((1,), (1,)), ((), ())),
                            preferred_element_type=F32)
        u = s - cnc_ref[...] + cnr_ref[...]
        ok = (row_i % N_HEADS == col_i % N_HEADS) & (col_i // N_HEADS <= row_i // N_HEADS)
        u = jnp.where(ok, u, -jnp.inf)
        m = jnp.max(u, axis=1, keepdims=True)
        p = jnp.exp(u - m)
        m_ref[...] = m
        l_ref[...] = jnp.sum(p, axis=1, keepdims=True)
        acc_ref[...] = jnp.dot(p, vn_ref[...], preferred_element_type=F32)
        carry_ref[...] = jnp.zeros_like(carry_ref)

    lf = jnp.concatenate([r[...] for r in lf_pages], axis=0)
    lane = lax.broadcasted_iota(jnp.int32, lf.shape, 1)
    incl = lf
    tot = lf
    step = N_HEADS
    while step < PAGE_ROWS:
        incl = incl + jnp.where(lane < PAGE_ROWS - step, pltpu.roll(incl, PAGE_ROWS - step, axis=1), 0.0)
        tot = tot + pltpu.roll(tot, step, axis=1)
        step *= 2
    excl = incl - lf
    run = carry_ref[...]
    rests = [None] * n_grp
    for g in reversed(range(n_grp)):
        rests[g] = excl[g:g + 1, :] + run
        run = run + tot[g:g + 1, :]
    carry_ref[...] = run

    qb = q_ref[...]
    same_head = (lax.broadcasted_iota(jnp.int32, (n_rows, PAGE_ROWS), 0) % N_HEADS
                 == lax.broadcasted_iota(jnp.int32, (n_rows, PAGE_ROWS), 1) % N_HEADS)
    cnr = cnr_ref[...]
    us = []
    for g in range(n_grp):
        kf = k_pages[g][...].reshape(PAGE_ROWS, HEAD_DIM).astype(BF16)
        s = lax.dot_general(qb, kf, (((1,), (1,)), ((), ())), preferred_element_type=F32)
        us.append(jnp.where(same_head, s + rests[g] + cnr, -jnp.inf))
    m_old = m_ref[...]
    m_new = m_old
    for u in us:
        m_new = jnp.maximum(m_new, jnp.max(u, axis=1, keepdims=True))
    alpha = jnp.exp(m_old - m_new)
    l_new = alpha * l_ref[...]
    pv = alpha * acc_ref[...]
    for g in range(n_grp):
        p = jnp.exp(us[g] - m_new)
        l_new = l_new + jnp.sum(p, axis=1, keepdims=True)
        vf = v_pages[g][...].reshape(PAGE_ROWS, HEAD_DIM).astype(BF16)
        pv = pv + jnp.dot(p.astype(BF16), vf, preferred_element_type=F32)
    m_ref[...] = m_new
    l_ref[...] = l_new
    acc_ref[...] = pv

    @pl.when(j == pl.num_programs(1) - 1)
    def _():
        o_ref[...] = (acc_ref[...] / l_ref[...]).astype(o_ref.dtype)


def _fox_sample(qf, knf, vnf, cn_rows, cn_cols, cache_k, cache_v, cache_lf_rows, page_table,
                layer, n_grp):
    s_, n_rows, _ = qf.shape
    n_pages = page_table.shape[1]
    n_steps = n_pages // n_grp

    def page_of(s, j, pt, g):
        return pt[s * n_pages + (n_steps - 1 - j) * n_grp + g]

    def kv_map(g):
        return lambda s, j, pt: (layer, page_of(s, j, pt, g), 0, 0, 0)

    def lf_map(g):
        return lambda s, j, pt: (page_of(s, j, pt, g), 0, 0)

    seq3 = lambda s, j, pt: (s, 0, 0)
    kv_block = (None, None, PAGE_SIZE, N_HEADS, HEAD_DIM)
    in_specs = [pl.BlockSpec((None, n_rows, HEAD_DIM), seq3),
                pl.BlockSpec((None, n_rows, HEAD_DIM), seq3),
                pl.BlockSpec((None, n_rows, HEAD_DIM), seq3),
                pl.BlockSpec((None, n_rows, 1), seq3),
                pl.BlockSpec((None, 1, n_rows), seq3)]
    in_specs += [pl.BlockSpec(kv_block, kv_map(g)) for g in range(n_grp)]
    in_specs += [pl.BlockSpec(kv_block, kv_map(g)) for g in range(n_grp)]
    in_specs += [pl.BlockSpec((None, 1, PAGE_ROWS), lf_map(g)) for g in range(n_grp)]
    grid_spec = pltpu.PrefetchScalarGridSpec(
        num_scalar_prefetch=1,
        grid=(s_, n_steps),
        in_specs=in_specs,
        out_specs=pl.BlockSpec((None, n_rows, HEAD_DIM), seq3),
        scratch_shapes=[pltpu.VMEM((n_rows, 1), F32),
                        pltpu.VMEM((n_rows, 1), F32),
                        pltpu.VMEM((n_rows, HEAD_DIM), F32),
                        pltpu.VMEM((1, PAGE_ROWS), F32)])
    return pl.pallas_call(
        functools.partial(_fox_sample_kernel, n_grp=n_grp),
        grid_spec=grid_spec,
        out_shape=jax.ShapeDtypeStruct((s_, n_rows, HEAD_DIM), BF16),
        compiler_params=_params(2),
        name="fox_sample",
    )(page_table.reshape(-1), qf, knf, vnf, cn_rows, cn_cols,
      *([cache_k] * n_grp), *([cache_v] * n_grp), *([cache_lf_rows] * n_grp))


HALO = 32


def _conv_kernel(z_ref, hist_ref, w_ref, b_ref, g_ref, bb_ref, o_ref, st_ref, ext_ref, *, bt):
    t = pl.program_id(1)
    pad = HALO - (CONV_WIDTH - 1)

    @pl.when(t == 0)
    def _():
        ext_ref[0:pad, :] = jnp.zeros((pad, D_CONV), F32)
        ext_ref[pad:HALO, :] = hist_ref[...]

    @pl.when(t > 0)
    def _():
        ext_ref[0:HALO, :] = ext_ref[bt:bt + HALO, :]

    ext_ref[HALO:HALO + bt, :] = z_ref[...]
    acc = jnp.zeros((bt, D_CONV), F32) + b_ref[...]
    for j in range(CONV_WIDTH):
        acc = acc + w_ref[j:j + 1, :] * ext_ref[pl.ds(j + pad, bt), :]
    mu = jnp.mean(acc, axis=-1, keepdims=True)
    xc = acc - mu
    var = jnp.mean(xc * xc, axis=-1, keepdims=True)
    y = xc * lax.rsqrt(var + EPS) * g_ref[...] + bb_ref[...]
    o_ref[...] = (y * _sigmoid(y)).astype(o_ref.dtype)

    @pl.when(t == pl.num_programs(1) - 1)
    def _():
        st_ref[...] = ext_ref[bt + pad:bt + HALO, :]


def _conv(z3, hist, w_dw, b_dw, ln_g, ln_b, bt):
    n_seq, t, c = z3.shape
    k1 = CONV_WIDTH - 1
    return pl.pallas_call(
        functools.partial(_conv_kernel, bt=bt),
        grid=(n_seq, t // bt),
        in_specs=[pl.BlockSpec((None, bt, c), lambda i, j: (i, j, 0)),
                  pl.BlockSpec((None, k1, c), lambda i, j: (i, 0, 0)),
                  _const_spec((CONV_WIDTH, c)),
                  _const_spec((1, c)), _const_spec((1, c)), _const_spec((1, c))],
        out_specs=[pl.BlockSpec((None, bt, c), lambda i, j: (i, j, 0)),
                   pl.BlockSpec((None, k1, c), lambda i, j: (i, 0, 0))],
        out_shape=[jax.ShapeDtypeStruct((n_seq, t, c), BF16),
                   jax.ShapeDtypeStruct((n_seq, k1, c), F32)],
        scratch_shapes=[pltpu.VMEM((HALO + bt, c), F32)],
        compiler_params=_params(2),
        name="conv",
    )(z3, hist, w_dw, b_dw.reshape(1, c), ln_g.reshape(1, c), ln_b.reshape(1, c))


def _post_ffn_kernel(x_ref, oa_ref, oc_ref, mod_ref, hist_ref, wo_ref, gpm_ref, gpf_ref, gqf_ref,
                     wu_ref, cw_ref, cb_ref, wd_ref, y_ref, st_ref,
                     carry_ref, ext_ref, acc_ref, h2_ref):
    t = pl.program_id(1)
    bs, bt, d = x_ref.shape
    m = bs * bt
    ck = FFN_CHUNK
    k1 = FFN_CONV_WIDTH - 1

    mix = (jnp.dot(oa_ref[...], wo_ref[0:D_ATT, :], preferred_element_type=F32)
           + jnp.dot(oc_ref[...], wo_ref[D_ATT:, :], preferred_element_type=F32))
    x1 = x_ref[...] + mod_ref[:, 2:3, :] * (_rms(mix.reshape(bs, bt, d)) * gpm_ref[...])
    y_ref[...] = x1
    h2 = _rms(x1) * gpf_ref[...] * (1.0 + mod_ref[:, 4:5, :]) + mod_ref[:, 3:4, :]
    h2_ref[...] = h2.reshape(m, d).astype(BF16)
    acc_ref[...] = jnp.zeros_like(acc_ref)

    @pl.when(t == 0)
    def _():
        carry_ref[...] = jnp.zeros_like(carry_ref)
        carry_ref[:, :, SUBLANES - k1:SUBLANES, :] = hist_ref[...]

    def conv_half(idx):
        up = jnp.dot(h2_ref[...], wu_ref[idx], preferred_element_type=F32).reshape(bs, bt, ck)
        ext_ref[:, 0:SUBLANES, :] = carry_ref[idx]
        ext_ref[:, SUBLANES:, :] = up
        carry_ref[idx] = up[:, bt - SUBLANES:, :]
        st_ref[idx] = up[:, bt - k1:, :]
        w = cw_ref[idx]
        return (w[0:1, :] * ext_ref[:, pl.ds(SUBLANES - 2, bt), :]
                + w[1:2, :] * ext_ref[:, pl.ds(SUBLANES - 1, bt), :]
                + w[2:3, :] * up + cb_ref[idx])

    def chunk(c, _):
        ua = conv_half(c)
        ub = conv_half(c + N_FFN_CHUNKS)
        g = (ua * _sigmoid(ua) * ub).reshape(m, ck).astype(BF16)
        acc_ref[...] += jnp.dot(g, wd_ref[c], preferred_element_type=F32)
        return 0

    lax.fori_loop(0, N_FFN_CHUNKS, chunk, 0)
    ffn = acc_ref[...].reshape(bs, bt, d)
    y_ref[...] = y_ref[...] + mod_ref[:, 5:6, :] * (_rms(ffn) * gqf_ref[...])


def _post_ffn(x3, o_att, o_conv, mod, hist_c, w_out, g_post_mix, g_pre_ffn, g_post_ffn,
              w_up_c, cw_c, cb_c, w_down_c, bs, bt):
    n_seq, t, d = x3.shape
    m = bs * bt
    nt = t // bt
    k1 = FFN_CONV_WIDTH - 1
    nc2 = 2 * N_FFN_CHUNKS
    row = lambda i, j: (i * nt + j, 0)
    return pl.pallas_call(
        _post_ffn_kernel,
        grid=(n_seq // bs, nt),
        in_specs=[pl.BlockSpec((bs, bt, d), lambda i, j: (i, j, 0)),
                  pl.BlockSpec((m, D_ATT), row),
                  pl.BlockSpec((m, D_CONV), row),
                  pl.BlockSpec((bs, N_MOD, d), lambda i, j: (i, 0, 0)),
                  pl.BlockSpec((nc2, bs, k1, FFN_CHUNK), lambda i, j: (0, i, 0, 0)),
                  _const_spec((d, d)),
                  _const_spec((1, d)), _const_spec((1, d)), _const_spec((1, d)),
                  _const_spec((nc2, d, FFN_CHUNK)),
                  _const_spec((nc2, FFN_CONV_WIDTH, FFN_CHUNK)),
                  _const_spec((nc2, 1, FFN_CHUNK)),
                  _const_spec((N_FFN_CHUNKS, FFN_CHUNK, d))],
        out_specs=[pl.BlockSpec((bs, bt, d), lambda i, j: (i, j, 0)),
                   pl.BlockSpec((nc2, bs, k1, FFN_CHUNK), lambda i, j: (0, i, 0, 0))],
        out_shape=[jax.ShapeDtypeStruct((n_seq, t, d), F32),
                   jax.ShapeDtypeStruct((nc2, n_seq, k1, FFN_CHUNK), F32)],
        scratch_shapes=[pltpu.VMEM((nc2, bs, SUBLANES, FFN_CHUNK), F32),
                        pltpu.VMEM((bs, SUBLANES + bt, FFN_CHUNK), F32),
                        pltpu.VMEM((m, d), F32),
                        pltpu.VMEM((m, d), BF16)],
        compiler_params=_params(2),
        name="post_ffn",
    )(x3, o_att, o_conv, mod, hist_c, w_out, g_post_mix, g_pre_ffn, g_post_ffn,
      w_up_c, cw_c, cb_c, w_down_c)


def _chunk_cols(a):
    lead = a.shape[:-1]
    a = a.reshape(lead + (2 * N_FFN_CHUNKS, FFN_CHUNK))
    return jnp.moveaxis(a, -2, 0)


def _unchunk_cols(a):
    a = jnp.moveaxis(a, 0, -2)
    return a.reshape(a.shape[:-2] + (2 * D_FF,))


PROMPT_TILES = dict(mix=(1, 512), conv=256, ffn=(1, 256), tq=512)
SAMPLE_TILES = dict(mix=(32, 8), conv=8, ffn=(16, 8), n_grp=8)


def _layer(x3, mod, conv_hist, ffn_hist, attend, tiles, lw):
    n_seq, t, d = x3.shape
    q, kb, vb, k, v, lf, z = _mix_in(x3, mod, lw["g_pre_mix"], lw["w_in_r"], lw["bf_pad"], *tiles["mix"])
    o_att = attend(q, kb, vb, k, v, lf)
    o_conv, conv_state = _conv(z.reshape(n_seq, t, D_CONV), conv_hist, lw["w_dw"], lw["b_dw"],
                               lw["ln_g"], lw["ln_b"], tiles["conv"])
    y, ffn_state_c = _post_ffn(x3, o_att, o_conv.reshape(n_seq * t, D_CONV), mod, _chunk_cols(ffn_hist),
                               lw["w_out"], lw["g_post_mix"], lw["g_pre_ffn"], lw["g_post_ffn"],
                               lw["w_up_c"], lw["cw_c"], lw["cb_c"], lw["w_down_c"], *tiles["ffn"])
    return (y, k.reshape(n_seq, t, N_HEADS, HEAD_DIM), v.reshape(n_seq, t, N_HEADS, HEAD_DIM),
            lf.reshape(n_seq, t, N_HEADS), conv_state, _unchunk_cols(ffn_state_c))


def kernel(x_prompt, x_sample, cache_k, cache_v, cache_logf, state_conv, state_ffn, page_table, c_prompt, c_sample, w_ada, b_ada, g_pre_mix, g_post_mix, g_pre_ffn, g_post_ffn, w_in, b_f, w_dw, b_dw, ln_g, ln_b, w_out, w_up, w_ffn_dw, b_ffn_dw, w_down):
    depth = w_ada.shape[0]
    b, t, d = x_prompt.shape
    s_, t_new, _ = x_sample.shape
    n_phys = cache_k.shape[1]
    yp, ys = x_prompt, x_sample
    outs_p, outs_s = [], []
    c_all = jnp.concatenate([c_prompt, c_sample], axis=0)
    c_all = jnp.pad(c_all, ((0, (-c_all.shape[0]) % SUBLANES), (0, 0)))
    for l in range(depth):
        row = lambda a: a[l].reshape(1, -1)
        wi = w_in[l]
        w_in_r = jnp.concatenate(
            [wi[:, :3 * D_ATT], wi[:, 3 * D_ATT + N_HEADS:],
             jnp.pad(wi[:, 3 * D_ATT:3 * D_ATT + N_HEADS], ((0, 0), (0, LANES - N_HEADS)))],
            axis=1).astype(BF16)
        lw = dict(
            g_pre_mix=row(g_pre_mix), g_post_mix=row(g_post_mix),
            g_pre_ffn=row(g_pre_ffn), g_post_ffn=row(g_post_ffn),
            w_in_r=w_in_r, bf_pad=jnp.pad(b_f[l], (0, LANES - N_HEADS)).reshape(1, LANES),
            w_dw=w_dw[l], b_dw=b_dw[l], ln_g=ln_g[l], ln_b=ln_b[l],
            w_out=w_out[l].astype(BF16),
            w_up_c=_chunk_cols(w_up[l].astype(BF16)),
            cw_c=_chunk_cols(w_ffn_dw[l]),
            cb_c=_chunk_cols(b_ffn_dw[l].reshape(1, -1)),
            w_down_c=w_down[l].astype(BF16).reshape(N_FFN_CHUNKS, FFN_CHUNK, d),
        )
        mod = _ada(c_all, w_ada[l], b_ada[l])
        mod_p = mod[:b].reshape(b, N_MOD, d)
        mod_s = mod[b:b + s_].reshape(s_, N_MOD, d)

        def attend_p(q, kb, vb, k, v, lf):
            cum = _cumsum_lanes(lf.reshape(b, t, N_HEADS).transpose(0, 2, 1))
            cum_col = cum.reshape(b, N_HEADS // 2, 2, t).transpose(0, 1, 3, 2)
            return _fox_prompt(q, kb, vb, cum_col, b, t, PROMPT_TILES["tq"])

        def attend_s(q, kb, vb, k, v, lf):
            cn = _cumsum_rows(lf.reshape(s_, t_new, N_HEADS))
            n_rows = t_new * N_HEADS
            o = _fox_sample(q.reshape(s_, n_rows, HEAD_DIM), k.reshape(s_, n_rows, HEAD_DIM),
                            v.reshape(s_, n_rows, HEAD_DIM),
                            cn.reshape(s_, n_rows, 1), cn.reshape(s_, 1, n_rows),
                            cache_k, cache_v, cache_logf[l].reshape(n_phys, 1, PAGE_ROWS),
                            page_table, l, SAMPLE_TILES["n_grp"])
            return o.reshape(s_ * t_new, D_ATT)

        conv0 = jnp.zeros((b, CONV_WIDTH - 1, D_CONV), F32)
        ffn0 = jnp.zeros((b, FFN_CONV_WIDTH - 1, 2 * D_FF), F32)
        yp, *rp = _layer(yp, mod_p, conv0, ffn0, attend_p, PROMPT_TILES, lw)
        ys, *rs = _layer(ys, mod_s, state_conv[l], state_ffn[l], attend_s, SAMPLE_TILES, lw)
        outs_p.append(rp)
        outs_s.append(rs)
    stack = lambda outs, i: jnp.stack([o[i] for o in outs])
    return (yp, ys, *[stack(outs_p, i) for i in range(5)], *[stack(outs_s, i) for i in range(5)])
```

```python
import functools

import jax
import jax.numpy as jnp
from jax import lax
from jax.experimental import pallas as pl
from jax.experimental.pallas import tpu as pltpu

F32 = jnp.float32
BF16 = jnp.bfloat16

D_MODEL = 1024
HEAD_DIM = 64
N_HEADS = 8
D_ATT = N_HEADS * HEAD_DIM
D_CONV = D_MODEL - D_ATT
CONV_WIDTH = 31
D_FF = 2816
FFN_CONV_WIDTH = 3
N_MOD = 6
EPS = 1e-6
PAGE_SIZE = 128

LANES = 128
SUBLANES = 8
FFN_CHUNK = 256
N_FFN_CHUNKS = D_FF // FFN_CHUNK
D_IN_PAD = 3 * D_ATT + 2 * D_CONV + LANES
VMEM_LIMIT = 52 * 1024 * 1024


def _sigmoid(x):
    return 1.0 / (1.0 + jnp.exp(-x))


def _rms(x):
    return x * lax.rsqrt(jnp.mean(x * x, axis=-1, keepdims=True) + EPS)


def _params(n_grid):
    return pltpu.CompilerParams(dimension_semantics=("arbitrary",) * n_grid,
                                vmem_limit_bytes=VMEM_LIMIT)


def _const_spec(shape):
    nd = len(shape)
    return pl.BlockSpec(shape, lambda *_: (0,) * nd, pipeline_mode=pl.Buffered(1))


def _ada_kernel(c_ref, w_ref, b_ref, o_ref):
    c = c_ref[...]
    s = (c * _sigmoid(c)).astype(BF16)
    o_ref[...] = jnp.dot(s, w_ref[...].astype(BF16), preferred_element_type=F32) + b_ref[...]


def _ada(c_all, w_ada, b_ada):
    r = c_all.shape[0]
    n = w_ada.shape[1]
    bn = D_MODEL
    return pl.pallas_call(
        _ada_kernel,
        grid=(n // bn,),
        in_specs=[pl.BlockSpec((r, D_MODEL), lambda j: (0, 0)),
                  pl.BlockSpec((D_MODEL, bn), lambda j: (0, j)),
                  pl.BlockSpec((1, bn), lambda j: (0, j))],
        out_specs=pl.BlockSpec((r, bn), lambda j: (0, j)),
        out_shape=jax.ShapeDtypeStruct((r, n), F32),
        compiler_params=_params(1),
        name="ada",
    )(c_all, w_ada, b_ada.reshape(1, n))


def _mix_in_kernel(x_ref, mod_ref, g_ref, w_ref, bf_ref,
                   q_ref, kb_ref, vb_ref, k_ref, v_ref, lf_ref, z_ref):
    x = x_ref[...]
    bs, bt, d = x.shape
    h = _rms(x) * g_ref[...] * (1.0 + mod_ref[:, 1:2, :]) + mod_ref[:, 0:1, :]
    h = h.reshape(bs * bt, d).astype(BF16)
    p = jnp.dot(h, w_ref[...], preferred_element_type=F32)
    q = p[:, 0:D_ATT]
    k = p[:, D_ATT:2 * D_ATT]
    v = p[:, 2 * D_ATT:3 * D_ATT]
    a = p[:, 3 * D_ATT:3 * D_ATT + D_CONV]
    g = p[:, 3 * D_ATT + D_CONV:3 * D_ATT + 2 * D_CONV]
    f = p[:, 3 * D_ATT + 2 * D_CONV:] + bf_ref[...]
    q_ref[...] = (q * (HEAD_DIM ** -0.5)).astype(BF16)
    kb_ref[...] = k.astype(BF16)
    vb_ref[...] = v.astype(BF16)
    k_ref[...] = k
    v_ref[...] = v
    lf = jnp.minimum(f, 0.0) - jnp.log(1.0 + jnp.exp(-jnp.abs(f)))
    lf_ref[...] = lf[:, 0:N_HEADS]
    z_ref[...] = a * _sigmoid(g)


def _mix_in(x3, mod, g_pre, w_in_r, bf_pad, bs, bt):
    n_seq, t, d = x3.shape
    n_tok = n_seq * t
    m = bs * bt
    nt = t // bt
    row = lambda i, j: (i * nt + j, 0)
    tok = lambda c, dt: jax.ShapeDtypeStruct((n_tok, c), dt)
    return pl.pallas_call(
        _mix_in_kernel,
        grid=(n_seq // bs, nt),
        in_specs=[pl.BlockSpec((bs, bt, d), lambda i, j: (i, j, 0)),
                  pl.BlockSpec((bs, N_MOD, d), lambda i, j: (i, 0, 0)),
                  _const_spec((1, d)),
                  _const_spec((d, D_IN_PAD)),
                  _const_spec((1, LANES))],
        out_specs=[pl.BlockSpec((m, D_ATT), row)] * 5
                  + [pl.BlockSpec((m, N_HEADS), row), pl.BlockSpec((m, D_CONV), row)],
        out_shape=[tok(D_ATT, BF16), tok(D_ATT, BF16), tok(D_ATT, BF16),
                   tok(D_ATT, F32), tok(D_ATT, F32), tok(N_HEADS, F32), tok(D_CONV, F32)],
        compiler_params=_params(2),
        name="mix_in",
    )(x3, mod, g_pre, w_in_r, bf_pad)


def _cumsum_lanes_kernel(x_ref, o_ref):
    x = x_ref[...]
    t = x.shape[-1]
    idx = lax.broadcasted_iota(jnp.int32, x.shape, 1)
    step = 1
    while step < t:
        x = x + jnp.where(idx >= step, pltpu.roll(x, step, axis=1), 0.0)
        step *= 2
    o_ref[...] = x


def _cumsum_lanes(x):
    b, h, t = x.shape
    return pl.pallas_call(
        _cumsum_lanes_kernel,
        grid=(b,),
        in_specs=[pl.BlockSpec((None, h, t), lambda i: (i, 0, 0))],
        out_specs=pl.BlockSpec((None, h, t), lambda i: (i, 0, 0)),
        out_shape=jax.ShapeDtypeStruct((b, h, t), F32),
        compiler_params=_params(1),
        name="cumsum_lanes",
    )(x)


def _cumsum_rows_kernel(x_ref, o_ref):
    x = x_ref[...]
    acc = x[:, 0:1, :]
    rows = [acc]
    for t in range(1, x.shape[1]):
        acc = acc + x[:, t:t + 1, :]
        rows.append(acc)
    o_ref[...] = jnp.concatenate(rows, axis=1)


def _cumsum_rows(x):
    return pl.pallas_call(
        _cumsum_rows_kernel,
        out_shape=jax.ShapeDtypeStruct(x.shape, F32),
        name="cumsum_rows",
    )(x)


def _split3(x):
    hi = x.astype(BF16).astype(F32)
    r = x - hi
    mid = r.astype(BF16).astype(F32)
    lo = (r - mid).astype(BF16).astype(F32)
    return hi, mid, lo


N_BIAS = 3


def _augment(x, own, lane, base, pieces, fill):
    out = jnp.where(own, x, 0.0)
    for i in range(N_BIAS):
        out = jnp.where(lane == base + i, pieces[i], out)
    return jnp.where((lane >= base + N_BIAS) & (lane < base + 2 * N_BIAS), fill, out)


def _fox_prompt_kernel(q_ref, k_ref, v_ref, cs_ref, ci_ref, o_ref,
                       kaug_ref, vt_ref, qt_ref, m_ref, acc_ref, *, tq):
    qi = pl.program_id(2)
    t = k_ref.shape[0]
    nk = t // tq
    lane = lax.broadcasted_iota(jnp.int32, (1, LANES), 1)
    sub_head = lax.broadcasted_iota(jnp.int32, (LANES, 1), 0) // HEAD_DIM

    @pl.when(qi == 0)
    def _():
        def prep(c, _):
            off = pl.multiple_of(c * tq, tq)
            kc = k_ref[pl.ds(off, tq), :].astype(F32)
            vt = v_ref[pl.ds(off, tq), :].astype(F32).T
            for h in range(2):
                own = (lane // HEAD_DIM) == h
                cs = jnp.broadcast_to(cs_ref[pl.ds(off, tq), h:h + 1], (tq, LANES))
                kaug = _augment(kc, own, lane, HEAD_DIM * (1 - h), _split3(cs), 1.0)
                kaug_ref[h, c] = kaug.astype(BF16)
                vt_ref[h, c] = jnp.where(sub_head == h, vt, 1.0).astype(BF16)
            return 0
        lax.fori_loop(0, nk, prep, 0)

    q = q_ref[...].astype(F32)
    for h in range(2):
        own = (lane // HEAD_DIM) == h
        ci = jnp.broadcast_to(ci_ref[:, h:h + 1], (tq, LANES))
        base = HEAD_DIM * (1 - h)
        qaug = _augment(q, own, lane, base + N_BIAS, _split3(ci), 0.0)
        qaug = jnp.where((lane >= base) & (lane < base + N_BIAS), -1.0, qaug)
        qt_ref[h] = qaug.T.astype(BF16)
    m_ref[...] = jnp.full(m_ref.shape, -jnp.inf, F32)
    acc_ref[...] = jnp.zeros_like(acc_ref)
    key_pos = lax.broadcasted_iota(jnp.int32, (tq, tq), 0)
    qry_pos = lax.broadcasted_iota(jnp.int32, (tq, tq), 1)

    def block(kb, masked):
        for h in range(2):
            ut = jnp.dot(kaug_ref[h, kb], qt_ref[h], preferred_element_type=F32)
            if masked:
                ut = jnp.where(key_pos <= qry_pos, ut, -jnp.inf)
            m_old = m_ref[h]
            m_new = jnp.maximum(m_old, jnp.max(ut, axis=0, keepdims=True))
            p = jnp.exp(ut - m_new).astype(BF16)
            m_ref[h] = m_new
            acc_ref[h] = (jnp.exp(m_old - m_new) * acc_ref[h]
                          + jnp.dot(vt_ref[h, kb], p, preferred_element_type=F32))

    def body(kb, _):
        block(kb, False)
        return 0

    lax.fori_loop(0, qi, body, 0)
    block(qi, True)
    a0 = acc_ref[0]
    a1 = acc_ref[1]
    ot = jnp.concatenate([a0[0:HEAD_DIM] / a0[HEAD_DIM:HEAD_DIM + 1],
                          a1[HEAD_DIM:] / a1[0:1]], axis=0)
    o_ref[...] = ot.T.astype(o_ref.dtype)


def _fox_prompt(q, kb, vb, cum_col, b, t, tq):
    nq = t // tq
    n_pairs = N_HEADS // 2
    return pl.pallas_call(
        functools.partial(_fox_prompt_kernel, tq=tq),
        grid=(b, n_pairs, nq),
        in_specs=[pl.BlockSpec((tq, LANES), lambda i, p, j: (i * nq + j, p)),
                  pl.BlockSpec((t, LANES), lambda i, p, j: (i, p)),
                  pl.BlockSpec((t, LANES), lambda i, p, j: (i, p)),
                  pl.BlockSpec((None, None, t, 2), lambda i, p, j: (i, p, 0, 0)),
                  pl.BlockSpec((None, None, tq, 2), lambda i, p, j: (i, p, j, 0))],
        out_specs=pl.BlockSpec((tq, LANES), lambda i, p, j: (i * nq + j, p)),
        out_shape=jax.ShapeDtypeStruct((b * t, D_ATT), BF16),
        scratch_shapes=[pltpu.VMEM((2, nq, tq, LANES), BF16),
                        pltpu.VMEM((2, nq, LANES, tq), BF16),
                        pltpu.VMEM((2, LANES, tq), BF16),
                        pltpu.VMEM((2, 1, tq), F32),
                        pltpu.VMEM((2, LANES, tq), F32)],
        compiler_params=_params(3),
        name="fox_prompt",
    )(q, kb, vb, cum_col, cum_col)


def _fox_sample_kernel(pt_ref, q_ref, knt_ref, vnt_ref, cnr_ref, cnc_ref, *rest, n_grp):
    del pt_ref
    k_pages = rest[0:n_grp]
    v_pages = rest[n_grp:2 * n_grp]
    lf_pages = rest[2 * n_grp:3 * n_grp]
    o_ref = rest[3 * n_grp]
    qbd_ref, m_ref, l_ref, acc_ref, carry_ref = rest[3 * n_grp + 1:]
    j = pl.program_id(1)
    t_new = q_ref.shape[0]
    n_rows = t_new * N_HEADS
    n_keys = n_grp * PAGE_SIZE
    nt_dims = (((1,), (1,)), ((), ()))
    head_of_lane = lax.broadcasted_iota(jnp.int32, (N_HEADS, D_ATT), 1) // HEAD_DIM
    head_mask = (head_of_lane == lax.broadcasted_iota(jnp.int32, (N_HEADS, D_ATT), 0)).astype(F32)

    @pl.when(j == 0)
    def _():
        q = q_ref[...].astype(F32)
        qbd = (q[:, None, :] * head_mask[None]).reshape(n_rows, D_ATT)
        qbd_ref[...] = qbd.astype(BF16)
        s = jnp.dot(qbd, knt_ref[...], preferred_element_type=F32)
        u = (s.reshape(t_new, N_HEADS, t_new) - cnc_ref[...][None]).reshape(n_rows, t_new)
        u = u + cnr_ref[...]
        tok = lax.broadcasted_iota(jnp.int32, (n_rows, t_new), 0) // N_HEADS
        key = lax.broadcasted_iota(jnp.int32, (n_rows, t_new), 1)
        u = jnp.where(key <= tok, u, -jnp.inf)
        m = jnp.max(u, axis=1, keepdims=True)
        p = jnp.exp(u - m)
        m_ref[...] = m
        acc_ref[...] = lax.dot_general(vnt_ref[...], p, nt_dims, preferred_element_type=F32)
        l_ref[...] = lax.dot_general(jnp.ones((l_ref.shape[0], t_new), F32), p, nt_dims,
                                     preferred_element_type=F32)
        carry_ref[...] = jnp.zeros_like(carry_ref)

    later = (lax.broadcasted_iota(jnp.int32, (PAGE_SIZE, PAGE_SIZE), 0)
             > lax.broadcasted_iota(jnp.int32, (PAGE_SIZE, PAGE_SIZE), 1)).astype(F32)
    carry = carry_ref[...]
    rests = [None] * n_grp
    for g in reversed(range(n_grp)):
        lf = lf_pages[g][...]
        rests[g] = jnp.dot(lf, later, precision=lax.Precision.HIGHEST,
                           preferred_element_type=F32) + carry
        carry = carry + jnp.sum(lf, axis=1, keepdims=True)
    carry_ref[...] = carry
    rest_all = jnp.concatenate(rests, axis=1)

    kt = jnp.concatenate([r[...].reshape(D_ATT, PAGE_SIZE).astype(BF16) for r in k_pages], axis=1)
    s = jnp.dot(qbd_ref[...], kt, preferred_element_type=F32)
    u = (s.reshape(t_new, N_HEADS, n_keys) + rest_all[None]).reshape(n_rows, n_keys)
    u = u + cnr_ref[...]
    m_old = m_ref[...]
    m_new = jnp.maximum(m_old, jnp.max(u, axis=1, keepdims=True))
    m_ref[...] = m_new
    p = jnp.exp(u - m_new).astype(BF16)
    eye = (lax.broadcasted_iota(jnp.int32, (n_rows, n_rows), 0)
           == lax.broadcasted_iota(jnp.int32, (n_rows, n_rows), 1))
    alpha = jnp.sum(jnp.where(eye, jnp.exp(m_old - m_new), 0.0), axis=0, keepdims=True)
    vt = jnp.concatenate([r[...].reshape(D_ATT, PAGE_SIZE).astype(BF16) for r in v_pages], axis=1)
    acc_ref[...] = alpha * acc_ref[...] + lax.dot_general(vt, p, nt_dims, preferred_element_type=F32)
    l_ref[...] = alpha * l_ref[...] + lax.dot_general(jnp.ones((l_ref.shape[0], n_keys), BF16), p,
                                                      nt_dims, preferred_element_type=F32)

    @pl.when(j == pl.num_programs(1) - 1)
    def _():
        o_t = acc_ref[...] / l_ref[0:1, :]
        keep = (lax.broadcasted_iota(jnp.int32, o_t.shape, 0) // HEAD_DIM
                == lax.broadcasted_iota(jnp.int32, o_t.shape, 1) % N_HEADS)
        gather = (lax.broadcasted_iota(jnp.int32, (n_rows, t_new), 0) // N_HEADS
                  == lax.broadcasted_iota(jnp.int32, (n_rows, t_new), 1)).astype(F32)
        o_ref[...] = jnp.dot(jnp.where(keep, o_t, 0.0), gather,
                             preferred_element_type=F32).astype(o_ref.dtype)


def _fox_sample(q3, knt, vnt, cn_rows, cn_cols, cache_kt, cache_vt, cache_lft, page_table,
                layer, n_grp):
    s_, t_new, _ = q3.shape
    n_rows = t_new * N_HEADS
    n_pages = page_table.shape[1]
    n_steps = n_pages // n_grp

    def page_of(s, j, pt, g):
        return pt[s * n_pages + (n_steps - 1 - j) * n_grp + g]

    def kv_map(g):
        return lambda s, j, pt: (layer, page_of(s, j, pt, g), 0, 0, 0)

    def lf_map(g):
        return lambda s, j, pt: (layer, page_of(s, j, pt, g), 0, 0)

    seq3 = lambda s, j, pt: (s, 0, 0)
    kv_block = (None, None, N_HEADS, HEAD_DIM, PAGE_SIZE)
    in_specs = [pl.BlockSpec((None, t_new, D_ATT), seq3),
                pl.BlockSpec((None, D_ATT, t_new), seq3),
                pl.BlockSpec((None, D_ATT, t_new), seq3),
                pl.BlockSpec((None, n_rows, 1), seq3),
                pl.BlockSpec((None, N_HEADS, t_new), seq3)]
    in_specs += [pl.BlockSpec(kv_block, kv_map(g)) for g in range(n_grp)]
    in_specs += [pl.BlockSpec(kv_block, kv_map(g)) for g in range(n_grp)]
    in_specs += [pl.BlockSpec((None, None, N_HEADS, PAGE_SIZE), lf_map(g)) for g in range(n_grp)]
    grid_spec = pltpu.PrefetchScalarGridSpec(
        num_scalar_prefetch=1,
        grid=(s_, n_steps),
        in_specs=in_specs,
        out_specs=pl.BlockSpec((None, D_ATT, t_new), seq3),
        scratch_shapes=[pltpu.VMEM((n_rows, D_ATT), BF16),
                        pltpu.VMEM((n_rows, 1), F32),
                        pltpu.VMEM((2 * SUBLANES, n_rows), F32),
                        pltpu.VMEM((D_ATT, n_rows), F32),
                        pltpu.VMEM((N_HEADS, 1), F32)])
    return pl.pallas_call(
        functools.partial(_fox_sample_kernel, n_grp=n_grp),
        grid_spec=grid_spec,
        out_shape=jax.ShapeDtypeStruct((s_, D_ATT, t_new), BF16),
        compiler_params=_params(2),
        name="fox_sample",
    )(page_table.reshape(-1), q3, knt, vnt, cn_rows, cn_cols,
      *([cache_kt] * n_grp), *([cache_vt] * n_grp), *([cache_lft] * n_grp))


HALO = 32


def _conv_kernel(z_ref, hist_ref, w_ref, b_ref, g_ref, bb_ref, o_ref, st_ref, ext_ref, *, bt):
    t = pl.program_id(1)
    pad = HALO - (CONV_WIDTH - 1)

    @pl.when(t == 0)
    def _():
        ext_ref[0:pad, :] = jnp.zeros((pad, D_CONV), F32)
        ext_ref[pad:HALO, :] = hist_ref[...]

    @pl.when(t > 0)
    def _():
        ext_ref[0:HALO, :] = ext_ref[bt:bt + HALO, :]

    ext_ref[HALO:HALO + bt, :] = z_ref[...]
    acc = jnp.zeros((bt, D_CONV), F32) + b_ref[...]
    for j in range(CONV_WIDTH):
        acc = acc + w_ref[j:j + 1, :] * ext_ref[pl.ds(j + pad, bt), :]
    mu = jnp.mean(acc, axis=-1, keepdims=True)
    xc = acc - mu
    var = jnp.mean(xc * xc, axis=-1, keepdims=True)
    y = xc * lax.rsqrt(var + EPS) * g_ref[...] + bb_ref[...]
    o_ref[...] = (y * _sigmoid(y)).astype(o_ref.dtype)

    @pl.when(t == pl.num_programs(1) - 1)
    def _():
        st_ref[...] = ext_ref[bt + pad:bt + HALO, :]


def _conv(z3, hist, w_dw, b_dw, ln_g, ln_b, bt):
    n_seq, t, c = z3.shape
    k1 = CONV_WIDTH - 1
    return pl.pallas_call(
        functools.partial(_conv_kernel, bt=bt),
        grid=(n_seq, t // bt),
        in_specs=[pl.BlockSpec((None, bt, c), lambda i, j: (i, j, 0)),
                  pl.BlockSpec((None, k1, c), lambda i, j: (i, 0, 0)),
                  _const_spec((CONV_WIDTH, c)),
                  _const_spec((1, c)), _const_spec((1, c)), _const_spec((1, c))],
        out_specs=[pl.BlockSpec((None, bt, c), lambda i, j: (i, j, 0)),
                   pl.BlockSpec((None, k1, c), lambda i, j: (i, 0, 0))],
        out_shape=[jax.ShapeDtypeStruct((n_seq, t, c), BF16),
                   jax.ShapeDtypeStruct((n_seq, k1, c), F32)],
        scratch_shapes=[pltpu.VMEM((HALO + bt, c), F32)],
        compiler_params=_params(2),
        name="conv",
    )(z3, hist, w_dw, b_dw.reshape(1, c), ln_g.reshape(1, c), ln_b.reshape(1, c))


def _post_ffn_kernel(x_ref, oa_ref, oc_ref, mod_ref, hist_ref, wo_ref, gpm_ref, gpf_ref, gqf_ref,
                     wu_ref, cw_ref, cb_ref, wd_ref, y_ref, st_ref,
                     carry_ref, ext_ref, acc_ref, h2_ref):
    t = pl.program_id(1)
    bs, bt, d = x_ref.shape
    m = bs * bt
    ck = FFN_CHUNK
    k1 = FFN_CONV_WIDTH - 1

    mix = (jnp.dot(oa_ref[...], wo_ref[0:D_ATT, :], preferred_element_type=F32)
           + jnp.dot(oc_ref[...], wo_ref[D_ATT:, :], preferred_element_type=F32))
    x1 = x_ref[...] + mod_ref[:, 2:3, :] * (_rms(mix.reshape(bs, bt, d)) * gpm_ref[...])
    y_ref[...] = x1
    h2 = _rms(x1) * gpf_ref[...] * (1.0 + mod_ref[:, 4:5, :]) + mod_ref[:, 3:4, :]
    h2_ref[...] = h2.reshape(m, d).astype(BF16)
    acc_ref[...] = jnp.zeros_like(acc_ref)

    @pl.when(t == 0)
    def _():
        carry_ref[...] = jnp.zeros_like(carry_ref)
        carry_ref[:, :, SUBLANES - k1:SUBLANES, :] = hist_ref[...]

    def conv_half(idx):
        up = jnp.dot(h2_ref[...], wu_ref[idx], preferred_element_type=F32).reshape(bs, bt, ck)
        ext_ref[:, 0:SUBLANES, :] = carry_ref[idx]
        ext_ref[:, SUBLANES:, :] = up
        carry_ref[idx] = up[:, bt - SUBLANES:, :]
        st_ref[idx] = up[:, bt - k1:, :]
        w = cw_ref[idx]
        return (w[0:1, :] * ext_ref[:, pl.ds(SUBLANES - 2, bt), :]
                + w[1:2, :] * ext_ref[:, pl.ds(SUBLANES - 1, bt), :]
                + w[2:3, :] * up + cb_ref[idx])

    def chunk(c, _):
        ua = conv_half(c)
        ub = conv_half(c + N_FFN_CHUNKS)
        g = (ua * _sigmoid(ua) * ub).reshape(m, ck).astype(BF16)
        acc_ref[...] += jnp.dot(g, wd_ref[c], preferred_element_type=F32)
        return 0

    lax.fori_loop(0, N_FFN_CHUNKS, chunk, 0)
    ffn = acc_ref[...].reshape(bs, bt, d)
    y_ref[...] = y_ref[...] + mod_ref[:, 5:6, :] * (_rms(ffn) * gqf_ref[...])


def _post_ffn(x3, o_att, o_conv, mod, hist_c, w_out, g_post_mix, g_pre_ffn, g_post_ffn,
              w_up_c, cw_c, cb_c, w_down_c, bs, bt):
    n_seq, t, d = x3.shape
    m = bs * bt
    nt = t // bt
    k1 = FFN_CONV_WIDTH - 1
    nc2 = 2 * N_FFN_CHUNKS
    row = lambda i, j: (i * nt + j, 0)
    return pl.pallas_call(
        _post_ffn_kernel,
        grid=(n_seq // bs, nt),
        in_specs=[pl.BlockSpec((bs, bt, d), lambda i, j: (i, j, 0)),
                  pl.BlockSpec((m, D_ATT), row),
                  pl.BlockSpec((m, D_CONV), row),
                  pl.BlockSpec((bs, N_MOD, d), lambda i, j: (i, 0, 0)),
                  pl.BlockSpec((nc2, bs, k1, FFN_CHUNK), lambda i, j: (0, i, 0, 0)),
                  _const_spec((d, d)),
                  _const_spec((1, d)), _const_spec((1, d)), _const_spec((1, d)),
                  _const_spec((nc2, d, FFN_CHUNK)),
                  _const_spec((nc2, FFN_CONV_WIDTH, FFN_CHUNK)),
                  _const_spec((nc2, 1, FFN_CHUNK)),
                  _const_spec((N_FFN_CHUNKS, FFN_CHUNK, d))],
        out_specs=[pl.BlockSpec((bs, bt, d), lambda i, j: (i, j, 0)),
                   pl.BlockSpec((nc2, bs, k1, FFN_CHUNK), lambda i, j: (0, i, 0, 0))],
        out_shape=[jax.ShapeDtypeStruct((n_seq, t, d), F32),
                   jax.ShapeDtypeStruct((nc2, n_seq, k1, FFN_CHUNK), F32)],
        scratch_shapes=[pltpu.VMEM((nc2, bs, SUBLANES, FFN_CHUNK), F32),
                        pltpu.VMEM((bs, SUBLANES + bt, FFN_CHUNK), F32),
                        pltpu.VMEM((m, d), F32),
                        pltpu.VMEM((m, d), BF16)],
        compiler_params=_params(2),
        name="post_ffn",
    )(x3, o_att, o_conv, mod, hist_c, w_out, g_post_mix, g_pre_ffn, g_post_ffn,
      w_up_c, cw_c, cb_c, w_down_c)


def _chunk_cols(a):
    lead = a.shape[:-1]
    a = a.reshape(lead + (2 * N_FFN_CHUNKS, FFN_CHUNK))
    return jnp.moveaxis(a, -2, 0)


def _unchunk_cols(a):
    a = jnp.moveaxis(a, 0, -2)
    return a.reshape(a.shape[:-2] + (2 * D_FF,))


PROMPT_TILES = dict(mix=(1, 512), conv=256, ffn=(1, 512), tq=512)
SAMPLE_TILES = dict(mix=(32, 8), conv=8, ffn=(16, 8), n_grp=16)


def _layer(x3, mod, conv_hist, ffn_hist, attend, tiles, lw):
    n_seq, t, d = x3.shape
    q, kb, vb, k, v, lf, z = _mix_in(x3, mod, lw["g_pre_mix"], lw["w_in_r"], lw["bf_pad"], *tiles["mix"])
    o_att = attend(q, kb, vb, k, v, lf)
    o_conv, conv_state = _conv(z.reshape(n_seq, t, D_CONV), conv_hist, lw["w_dw"], lw["b_dw"],
                               lw["ln_g"], lw["ln_b"], tiles["conv"])
    y, ffn_state_c = _post_ffn(x3, o_att, o_conv.reshape(n_seq * t, D_CONV), mod, _chunk_cols(ffn_hist),
                               lw["w_out"], lw["g_post_mix"], lw["g_pre_ffn"], lw["g_post_ffn"],
                               lw["w_up_c"], lw["cw_c"], lw["cb_c"], lw["w_down_c"], *tiles["ffn"])
    return (y, k.reshape(n_seq, t, N_HEADS, HEAD_DIM), v.reshape(n_seq, t, N_HEADS, HEAD_DIM),
            lf.reshape(n_seq, t, N_HEADS), conv_state, _unchunk_cols(ffn_state_c))


def kernel(x_prompt, x_sample, cache_k, cache_v, cache_logf, state_conv, state_ffn, page_table, c_prompt, c_sample, w_ada, b_ada, g_pre_mix, g_post_mix, g_pre_ffn, g_post_ffn, w_in, b_f, w_dw, b_dw, ln_g, ln_b, w_out, w_up, w_ffn_dw, b_ffn_dw, w_down):
    depth = w_ada.shape[0]
    b, t, d = x_prompt.shape
    s_, t_new, _ = x_sample.shape
    n_phys = cache_k.shape[1]
    yp, ys = x_prompt, x_sample
    outs_p, outs_s = [], []
    c_all = jnp.concatenate([c_prompt, c_sample], axis=0)
    c_all = jnp.pad(c_all, ((0, (-c_all.shape[0]) % SUBLANES), (0, 0)))
    for l in range(depth):
        row = lambda a: a[l].reshape(1, -1)
        wi = w_in[l]
        w_in_r = jnp.concatenate(
            [wi[:, :3 * D_ATT], wi[:, 3 * D_ATT + N_HEADS:],
             jnp.pad(wi[:, 3 * D_ATT:3 * D_ATT + N_HEADS], ((0, 0), (0, LANES - N_HEADS)))],
            axis=1).astype(BF16)
        lw = dict(
            g_pre_mix=row(g_pre_mix), g_post_mix=row(g_post_mix),
            g_pre_ffn=row(g_pre_ffn), g_post_ffn=row(g_post_ffn),
            w_in_r=w_in_r, bf_pad=jnp.pad(b_f[l], (0, LANES - N_HEADS)).reshape(1, LANES),
            w_dw=w_dw[l], b_dw=b_dw[l], ln_g=ln_g[l], ln_b=ln_b[l],
            w_out=w_out[l].astype(BF16),
            w_up_c=_chunk_cols(w_up[l].astype(BF16)),
            cw_c=_chunk_cols(w_ffn_dw[l]),
            cb_c=_chunk_cols(b_ffn_dw[l].reshape(1, -1)),
            w_down_c=w_down[l].astype(BF16).reshape(N_FFN_CHUNKS, FFN_CHUNK, d),
        )
        mod = _ada(c_all, w_ada[l], b_ada[l])
        mod_p = mod[:b].reshape(b, N_MOD, d)
        mod_s = mod[b:b + s_].reshape(s_, N_MOD, d)

        def attend_p(q, kb, vb, k, v, lf):
            cum = _cumsum_lanes(lf.reshape(b, t, N_HEADS).transpose(0, 2, 1))
            cum_col = cum.reshape(b, N_HEADS // 2, 2, t).transpose(0, 1, 3, 2)
            return _fox_prompt(q, kb, vb, cum_col, b, t, PROMPT_TILES["tq"])

        def attend_s(q, kb, vb, k, v, lf):
            cn = _cumsum_rows(lf.reshape(s_, t_new, N_HEADS))
            o_t = _fox_sample(q.reshape(s_, t_new, D_ATT),
                              k.reshape(s_, t_new, D_ATT).transpose(0, 2, 1),
                              v.reshape(s_, t_new, D_ATT).transpose(0, 2, 1),
                              cn.reshape(s_, t_new * N_HEADS, 1), cn.transpose(0, 2, 1),
                              cache_k.transpose(0, 1, 3, 4, 2), cache_v.transpose(0, 1, 3, 4, 2),
                              cache_logf.transpose(0, 1, 3, 2), page_table, l, SAMPLE_TILES["n_grp"])
            return o_t.transpose(0, 2, 1).reshape(s_ * t_new, D_ATT)

        conv0 = jnp.zeros((b, CONV_WIDTH - 1, D_CONV), F32)
        ffn0 = jnp.zeros((b, FFN_CONV_WIDTH - 1, 2 * D_FF), F32)
        yp, *rp = _layer(yp, mod_p, conv0, ffn0, attend_p, PROMPT_TILES, lw)
        ys, *rs = _layer(ys, mod_s, state_conv[l], state_ffn[l], attend_s, SAMPLE_TILES, lw)
        outs_p.append(rp)
        outs_s.append(rs)
    stack = lambda outs, i: jnp.stack([o[i] for o in outs])
    return (yp, ys, *[stack(outs_p, i) for i in range(5)], *[stack(outs_s, i) for i in range(5)])
```

```python
import functools

import jax
import jax.numpy as jnp
from jax import lax
from jax.experimental import pallas as pl
from jax.experimental.pallas import tpu as pltpu

F32 = jnp.float32
BF16 = jnp.bfloat16

D_MODEL = 1024
HEAD_DIM = 64
N_HEADS = 8
D_ATT = N_HEADS * HEAD_DIM
D_CONV = D_MODEL - D_ATT
CONV_WIDTH = 31
D_FF = 2816
FFN_CONV_WIDTH = 3
N_MOD = 6
EPS = 1e-6
PAGE_SIZE = 128

LANES = 128
SUBLANES = 8
FFN_CHUNK = 256
N_FFN_CHUNKS = D_FF // FFN_CHUNK
D_IN_PAD = 3 * D_ATT + 2 * D_CONV + LANES
VMEM_LIMIT = 52 * 1024 * 1024


def _sigmoid(x):
    return 1.0 / (1.0 + jnp.exp(-x))


def _rms(x):
    return x * lax.rsqrt(jnp.mean(x * x, axis=-1, keepdims=True) + EPS)


def _params(n_grid):
    return pltpu.CompilerParams(dimension_semantics=("arbitrary",) * n_grid,
                                vmem_limit_bytes=VMEM_LIMIT)


def _const_spec(shape):
    nd = len(shape)
    return pl.BlockSpec(shape, lambda *_: (0,) * nd, pipeline_mode=pl.Buffered(1))


def _ada_kernel(c_ref, w_ref, b_ref, o_ref):
    c = c_ref[...]
    s = (c * _sigmoid(c)).astype(BF16)
    o_ref[...] = jnp.dot(s, w_ref[...].astype(BF16), preferred_element_type=F32) + b_ref[...]


def _ada(c_all, w_ada, b_ada):
    r = c_all.shape[0]
    n = w_ada.shape[1]
    bn = D_MODEL
    return pl.pallas_call(
        _ada_kernel,
        grid=(n // bn,),
        in_specs=[pl.BlockSpec((r, D_MODEL), lambda j: (0, 0)),
                  pl.BlockSpec((D_MODEL, bn), lambda j: (0, j)),
                  pl.BlockSpec((1, bn), lambda j: (0, j))],
        out_specs=pl.BlockSpec((r, bn), lambda j: (0, j)),
        out_shape=jax.ShapeDtypeStruct((r, n), F32),
        compiler_params=_params(1),
        name="ada",
    )(c_all, w_ada, b_ada.reshape(1, n))


def _mix_in_kernel(x_ref, mod_ref, g_ref, w_ref, bf_ref,
                   q_ref, kb_ref, vb_ref, k_ref, v_ref, lf_ref, z_ref):
    x = x_ref[...]
    bs, bt, d = x.shape
    h = _rms(x) * g_ref[...] * (1.0 + mod_ref[:, 1:2, :]) + mod_ref[:, 0:1, :]
    h = h.reshape(bs * bt, d).astype(BF16)
    p = jnp.dot(h, w_ref[...], preferred_element_type=F32)
    q = p[:, 0:D_ATT]
    k = p[:, D_ATT:2 * D_ATT]
    v = p[:, 2 * D_ATT:3 * D_ATT]
    a = p[:, 3 * D_ATT:3 * D_ATT + D_CONV]
    g = p[:, 3 * D_ATT + D_CONV:3 * D_ATT + 2 * D_CONV]
    f = p[:, 3 * D_ATT + 2 * D_CONV:] + bf_ref[...]
    q_ref[...] = (q * (HEAD_DIM ** -0.5)).astype(BF16)
    kb_ref[...] = k.astype(BF16)
    vb_ref[...] = v.astype(BF16)
    k_ref[...] = k
    v_ref[...] = v
    lf = jnp.minimum(f, 0.0) - jnp.log(1.0 + jnp.exp(-jnp.abs(f)))
    lf_ref[...] = lf[:, 0:N_HEADS]
    z_ref[...] = a * _sigmoid(g)


def _mix_in(x3, mod, g_pre, w_in_r, bf_pad, bs, bt):
    n_seq, t, d = x3.shape
    n_tok = n_seq * t
    m = bs * bt
    nt = t // bt
    row = lambda i, j: (i * nt + j, 0)
    tok = lambda c, dt: jax.ShapeDtypeStruct((n_tok, c), dt)
    return pl.pallas_call(
        _mix_in_kernel,
        grid=(n_seq // bs, nt),
        in_specs=[pl.BlockSpec((bs, bt, d), lambda i, j: (i, j, 0)),
                  pl.BlockSpec((bs, N_MOD, d), lambda i, j: (i, 0, 0)),
                  _const_spec((1, d)),
                  _const_spec((d, D_IN_PAD)),
                  _const_spec((1, LANES))],
        out_specs=[pl.BlockSpec((m, D_ATT), row)] * 5
                  + [pl.BlockSpec((m, N_HEADS), row), pl.BlockSpec((m, D_CONV), row)],
        out_shape=[tok(D_ATT, BF16), tok(D_ATT, BF16), tok(D_ATT, BF16),
                   tok(D_ATT, F32), tok(D_ATT, F32), tok(N_HEADS, F32), tok(D_CONV, F32)],
        compiler_params=_params(2),
        name="mix_in",
    )(x3, mod, g_pre, w_in_r, bf_pad)


def _cumsum_lanes_kernel(x_ref, o_ref):
    x = x_ref[...]
    t = x.shape[-1]
    idx = lax.broadcasted_iota(jnp.int32, x.shape, 1)
    step = 1
    while step < t:
        x = x + jnp.where(idx >= step, pltpu.roll(x, step, axis=1), 0.0)
        step *= 2
    o_ref[...] = x


def _cumsum_lanes(x):
    b, h, t = x.shape
    return pl.pallas_call(
        _cumsum_lanes_kernel,
        grid=(b,),
        in_specs=[pl.BlockSpec((None, h, t), lambda i: (i, 0, 0))],
        out_specs=pl.BlockSpec((None, h, t), lambda i: (i, 0, 0)),
        out_shape=jax.ShapeDtypeStruct((b, h, t), F32),
        compiler_params=_params(1),
        name="cumsum_lanes",
    )(x)


def _cumsum_rows_kernel(x_ref, o_ref):
    x = x_ref[...]
    acc = x[:, 0:1, :]
    rows = [acc]
    for t in range(1, x.shape[1]):
        acc = acc + x[:, t:t + 1, :]
        rows.append(acc)
    o_ref[...] = jnp.concatenate(rows, axis=1)


def _cumsum_rows(x):
    return pl.pallas_call(
        _cumsum_rows_kernel,
        out_shape=jax.ShapeDtypeStruct(x.shape, F32),
        name="cumsum_rows",
    )(x)


def _split3(x):
    hi = x.astype(BF16).astype(F32)
    r = x - hi
    mid = r.astype(BF16).astype(F32)
    lo = (r - mid).astype(BF16).astype(F32)
    return hi, mid, lo


N_BIAS = 3


def _augment(x, own, lane, base, pieces, fill):
    out = jnp.where(own, x, 0.0)
    for i in range(N_BIAS):
        out = jnp.where(lane == base + i, pieces[i], out)
    return jnp.where((lane >= base + N_BIAS) & (lane < base + 2 * N_BIAS), fill, out)


def _fox_prompt_kernel(q_ref, k_ref, v_ref, cs_ref, ci_ref, o_ref,
                       kaug_ref, vt_ref, qt_ref, m_ref, acc_ref, ut_ref, *, tq):
    qi = pl.program_id(2)
    t = k_ref.shape[0]
    nk = t // tq
    lane = lax.broadcasted_iota(jnp.int32, (1, LANES), 1)
    sub_head = lax.broadcasted_iota(jnp.int32, (LANES, 1), 0) // HEAD_DIM

    @pl.when(qi == 0)
    def _():
        def prep(c, _):
            off = pl.multiple_of(c * tq, tq)
            kc = k_ref[pl.ds(off, tq), :].astype(F32)
            vt = v_ref[pl.ds(off, tq), :].astype(F32).T
            for h in range(2):
                own = (lane // HEAD_DIM) == h
                cs = jnp.broadcast_to(cs_ref[pl.ds(off, tq), h:h + 1], (tq, LANES))
                kaug = _augment(kc, own, lane, HEAD_DIM * (1 - h), _split3(cs), 1.0)
                kaug_ref[h, c] = kaug.astype(BF16)
                vt_ref[h, c] = jnp.where(sub_head == h, vt, 1.0).astype(BF16)
            return 0
        lax.fori_loop(0, nk, prep, 0)

    q = q_ref[...].astype(F32)
    for h in range(2):
        own = (lane // HEAD_DIM) == h
        ci = jnp.broadcast_to(ci_ref[:, h:h + 1], (tq, LANES))
        base = HEAD_DIM * (1 - h)
        qaug = _augment(q, own, lane, base + N_BIAS, _split3(ci), 0.0)
        qaug = jnp.where((lane >= base) & (lane < base + N_BIAS), -1.0, qaug)
        qt_ref[h] = qaug.T.astype(BF16)
    m_ref[...] = jnp.full(m_ref.shape, -jnp.inf, F32)
    acc_ref[...] = jnp.zeros_like(acc_ref)
    key_pos = lax.broadcasted_iota(jnp.int32, (tq, tq), 0)
    qry_pos = lax.broadcasted_iota(jnp.int32, (tq, tq), 1)

    def scores(kb):
        return [jnp.dot(kaug_ref[h, kb], qt_ref[h], preferred_element_type=F32) for h in range(2)]

    def block(kb, slot, masked, prefetch):
        uts = [ut_ref[slot, h] for h in range(2)]
        nxt = scores(kb + 1) if prefetch else None
        for h in range(2):
            ut = uts[h]
            if masked:
                ut = jnp.where(key_pos <= qry_pos, ut, -jnp.inf)
            m_old = m_ref[h]
            m_new = jnp.maximum(m_old, jnp.max(ut, axis=0, keepdims=True))
            p = jnp.exp(ut - m_new).astype(BF16)
            m_ref[h] = m_new
            acc_ref[h] = (jnp.exp(m_old - m_new) * acc_ref[h]
                          + jnp.dot(vt_ref[h, kb], p, preferred_element_type=F32))
        if prefetch:
            for h in range(2):
                ut_ref[1 - slot, h] = nxt[h]

    def body(i, _):
        block(2 * i, 0, False, True)
        block(2 * i + 1, 1, False, True)
        return 0

    first = scores(0)
    for h in range(2):
        ut_ref[0, h] = first[h]
    lax.fori_loop(0, qi // 2, body, 0)

    @pl.when(qi % 2 == 0)
    def _():
        block(qi, 0, True, False)

    @pl.when(qi % 2 == 1)
    def _():
        block(qi - 1, 0, False, True)
        block(qi, 1, True, False)

    a0 = acc_ref[0]
    a1 = acc_ref[1]
    ot = jnp.concatenate([a0[0:HEAD_DIM] / a0[HEAD_DIM:HEAD_DIM + 1],
                          a1[HEAD_DIM:] / a1[0:1]], axis=0)
    o_ref[...] = ot.T.astype(o_ref.dtype)


def _fox_prompt(q, kb, vb, cum_col, b, t, tq):
    nq = t // tq
    n_pairs = N_HEADS // 2
    return pl.pallas_call(
        functools.partial(_fox_prompt_kernel, tq=tq),
        grid=(b, n_pairs, nq),
        in_specs=[pl.BlockSpec((tq, LANES), lambda i, p, j: (i * nq + j, p)),
                  pl.BlockSpec((t, LANES), lambda i, p, j: (i, p)),
                  pl.BlockSpec((t, LANES), lambda i, p, j: (i, p)),
                  pl.BlockSpec((None, None, t, 2), lambda i, p, j: (i, p, 0, 0)),
                  pl.BlockSpec((None, None, tq, 2), lambda i, p, j: (i, p, j, 0))],
        out_specs=pl.BlockSpec((tq, LANES), lambda i, p, j: (i * nq + j, p)),
        out_shape=jax.ShapeDtypeStruct((b * t, D_ATT), BF16),
        scratch_shapes=[pltpu.VMEM((2, nq, tq, LANES), BF16),
                        pltpu.VMEM((2, nq, LANES, tq), BF16),
                        pltpu.VMEM((2, LANES, tq), BF16),
                        pltpu.VMEM((2, 1, tq), F32),
                        pltpu.VMEM((2, LANES, tq), F32),
                        pltpu.VMEM((2, 2, tq, tq), F32)],
        compiler_params=_params(3),
        name="fox_prompt",
    )(q, kb, vb, cum_col, cum_col)


def _fox_sample_kernel(pt_ref, q_ref, knt_ref, vnt_ref, cnr_ref, cnc_ref, *rest, n_grp):
    del pt_ref
    k_pages = rest[0:n_grp]
    v_pages = rest[n_grp:2 * n_grp]
    lf_pages = rest[2 * n_grp:3 * n_grp]
    o_ref = rest[3 * n_grp]
    qbd_ref, m_ref, l_ref, acc_ref, carry_ref = rest[3 * n_grp + 1:]
    j = pl.program_id(1)
    t_new = q_ref.shape[0]
    n_rows = t_new * N_HEADS
    n_keys = n_grp * PAGE_SIZE
    nt_dims = (((1,), (1,)), ((), ()))
    head_of_lane = lax.broadcasted_iota(jnp.int32, (N_HEADS, D_ATT), 1) // HEAD_DIM
    head_mask = (head_of_lane == lax.broadcasted_iota(jnp.int32, (N_HEADS, D_ATT), 0)).astype(F32)

    @pl.when(j == 0)
    def _():
        q = q_ref[...].astype(F32)
        qbd = (q[:, None, :] * head_mask[None]).reshape(n_rows, D_ATT)
        qbd_ref[...] = qbd.astype(BF16)
        s = jnp.dot(qbd, knt_ref[...], preferred_element_type=F32)
        u = (s.reshape(t_new, N_HEADS, t_new) - cnc_ref[...][None]).reshape(n_rows, t_new)
        u = u + cnr_ref[...]
        tok = lax.broadcasted_iota(jnp.int32, (n_rows, t_new), 0) // N_HEADS
        key = lax.broadcasted_iota(jnp.int32, (n_rows, t_new), 1)
        u = jnp.where(key <= tok, u, -jnp.inf)
        m = jnp.max(u, axis=1, keepdims=True)
        p = jnp.exp(u - m)
        m_ref[...] = m
        acc_ref[...] = lax.dot_general(vnt_ref[...], p, nt_dims, preferred_element_type=F32)
        l_ref[...] = lax.dot_general(jnp.ones((l_ref.shape[0], t_new), F32), p, nt_dims,
                                     preferred_element_type=F32)
        carry_ref[...] = jnp.zeros_like(carry_ref)

    src = lax.broadcasted_iota(jnp.int32, (PAGE_SIZE, 2 * PAGE_SIZE), 0)
    dst = lax.broadcasted_iota(jnp.int32, (PAGE_SIZE, 2 * PAGE_SIZE), 1)
    later = ((src > dst) | (dst >= PAGE_SIZE)).astype(F32)
    lf = jnp.concatenate([r[...] for r in lf_pages], axis=0)
    sums = jnp.dot(lf, later, precision=lax.Precision.HIGHEST, preferred_element_type=F32)
    run = carry_ref[...]
    rests = [None] * n_grp
    for g in reversed(range(n_grp)):
        rows = slice(g * N_HEADS, (g + 1) * N_HEADS)
        rests[g] = sums[rows, 0:PAGE_SIZE] + run
        run = run + sums[rows, PAGE_SIZE:]
    carry_ref[...] = run
    rest_all = jnp.concatenate(rests, axis=1)

    kt = jnp.concatenate([r[...].reshape(D_ATT, PAGE_SIZE).astype(BF16) for r in k_pages], axis=1)
    s = jnp.dot(qbd_ref[...], kt, preferred_element_type=F32)
    u = (s.reshape(t_new, N_HEADS, n_keys) + rest_all[None]).reshape(n_rows, n_keys)
    u = u + cnr_ref[...]
    m_old = m_ref[...]
    m_new = jnp.maximum(m_old, jnp.max(u, axis=1, keepdims=True))
    m_ref[...] = m_new
    p = jnp.exp(u - m_new).astype(BF16)
    eye = (lax.broadcasted_iota(jnp.int32, (n_rows, n_rows), 0)
           == lax.broadcasted_iota(jnp.int32, (n_rows, n_rows), 1))
    alpha = jnp.sum(jnp.where(eye, jnp.exp(m_old - m_new), 0.0), axis=0, keepdims=True)
    vt = jnp.concatenate([r[...].reshape(D_ATT, PAGE_SIZE).astype(BF16) for r in v_pages], axis=1)
    acc_ref[...] = alpha * acc_ref[...] + lax.dot_general(vt, p, nt_dims, preferred_element_type=F32)
    l_ref[...] = alpha * l_ref[...] + lax.dot_general(jnp.ones((l_ref.shape[0], n_keys), BF16), p,
                                                      nt_dims, preferred_element_type=F32)

    @pl.when(j == pl.num_programs(1) - 1)
    def _():
        o_t = acc_ref[...] / l_ref[0:1, :]
        keep = (lax.broadcasted_iota(jnp.int32, o_t.shape, 0) // HEAD_DIM
                == lax.broadcasted_iota(jnp.int32, o_t.shape, 1) % N_HEADS)
        gather = (lax.broadcasted_iota(jnp.int32, (n_rows, t_new), 0) // N_HEADS
                  == lax.broadcasted_iota(jnp.int32, (n_rows, t_new), 1)).astype(F32)
        o_ref[...] = jnp.dot(jnp.where(keep, o_t, 0.0), gather,
                             preferred_element_type=F32).astype(o_ref.dtype)


def _fox_sample(q3, knt, vnt, cn_rows, cn_cols, cache_kt, cache_vt, cache_lft, page_table,
                layer, n_grp):
    s_, t_new, _ = q3.shape
    n_rows = t_new * N_HEADS
    n_pages = page_table.shape[1]
    n_steps = n_pages // n_grp

    def page_of(s, j, pt, g):
        return pt[s * n_pages + (n_steps - 1 - j) * n_grp + g]

    def kv_map(g):
        return lambda s, j, pt: (layer, page_of(s, j, pt, g), 0, 0, 0)

    def lf_map(g):
        return lambda s, j, pt: (layer, page_of(s, j, pt, g), 0, 0)

    seq3 = lambda s, j, pt: (s, 0, 0)
    kv_block = (None, None, N_HEADS, HEAD_DIM, PAGE_SIZE)
    in_specs = [pl.BlockSpec((None, t_new, D_ATT), seq3),
                pl.BlockSpec((None, D_ATT, t_new), seq3),
                pl.BlockSpec((None, D_ATT, t_new), seq3),
                pl.BlockSpec((None, n_rows, 1), seq3),
                pl.BlockSpec((None, N_HEADS, t_new), seq3)]
    in_specs += [pl.BlockSpec(kv_block, kv_map(g)) for g in range(n_grp)]
    in_specs += [pl.BlockSpec(kv_block, kv_map(g)) for g in range(n_grp)]
    in_specs += [pl.BlockSpec((None, None, N_HEADS, PAGE_SIZE), lf_map(g)) for g in range(n_grp)]
    grid_spec = pltpu.PrefetchScalarGridSpec(
        num_scalar_prefetch=1,
        grid=(s_, n_steps),
        in_specs=in_specs,
        out_specs=pl.BlockSpec((None, D_ATT, t_new), seq3),
        scratch_shapes=[pltpu.VMEM((n_rows, D_ATT), BF16),
                        pltpu.VMEM((n_rows, 1), F32),
                        pltpu.VMEM((2 * SUBLANES, n_rows), F32),
                        pltpu.VMEM((D_ATT, n_rows), F32),
                        pltpu.VMEM((N_HEADS, PAGE_SIZE), F32)])
    return pl.pallas_call(
        functools.partial(_fox_sample_kernel, n_grp=n_grp),
        grid_spec=grid_spec,
        out_shape=jax.ShapeDtypeStruct((s_, D_ATT, t_new), BF16),
        compiler_params=_params(2),
        name="fox_sample",
    )(page_table.reshape(-1), q3, knt, vnt, cn_rows, cn_cols,
      *([cache_kt] * n_grp), *([cache_vt] * n_grp), *([cache_lft] * n_grp))


HALO = 32


def _conv_kernel(z_ref, hist_ref, w_ref, b_ref, g_ref, bb_ref, o_ref, st_ref, ext_ref, *, bt):
    t = pl.program_id(1)
    pad = HALO - (CONV_WIDTH - 1)

    @pl.when(t == 0)
    def _():
        ext_ref[0:pad, :] = jnp.zeros((pad, D_CONV), F32)
        ext_ref[pad:HALO, :] = hist_ref[...]

    @pl.when(t > 0)
    def _():
        ext_ref[0:HALO, :] = ext_ref[bt:bt + HALO, :]

    ext_ref[HALO:HALO + bt, :] = z_ref[...]
    acc = jnp.zeros((bt, D_CONV), F32) + b_ref[...]
    for j in range(CONV_WIDTH):
        acc = acc + w_ref[j:j + 1, :] * ext_ref[pl.ds(j + pad, bt), :]
    mu = jnp.mean(acc, axis=-1, keepdims=True)
    xc = acc - mu
    var = jnp.mean(xc * xc, axis=-1, keepdims=True)
    y = xc * lax.rsqrt(var + EPS) * g_ref[...] + bb_ref[...]
    o_ref[...] = (y * _sigmoid(y)).astype(o_ref.dtype)

    @pl.when(t == pl.num_programs(1) - 1)
    def _():
        st_ref[...] = ext_ref[bt + pad:bt + HALO, :]


def _conv(z3, hist, w_dw, b_dw, ln_g, ln_b, bt):
    n_seq, t, c = z3.shape
    k1 = CONV_WIDTH - 1
    return pl.pallas_call(
        functools.partial(_conv_kernel, bt=bt),
        grid=(n_seq, t // bt),
        in_specs=[pl.BlockSpec((None, bt, c), lambda i, j: (i, j, 0)),
                  pl.BlockSpec((None, k1, c), lambda i, j: (i, 0, 0)),
                  _const_spec((CONV_WIDTH, c)),
                  _const_spec((1, c)), _const_spec((1, c)), _const_spec((1, c))],
        out_specs=[pl.BlockSpec((None, bt, c), lambda i, j: (i, j, 0)),
                   pl.BlockSpec((None, k1, c), lambda i, j: (i, 0, 0))],
        out_shape=[jax.ShapeDtypeStruct((n_seq, t, c), BF16),
                   jax.ShapeDtypeStruct((n_seq, k1, c), F32)],
        scratch_shapes=[pltpu.VMEM((HALO + bt, c), F32)],
        compiler_params=_params(2),
        name="conv",
    )(z3, hist, w_dw, b_dw.reshape(1, c), ln_g.reshape(1, c), ln_b.reshape(1, c))


def _post_ffn_kernel(x_ref, oa_ref, oc_ref, mod_ref, hist_ref, wo_ref, gpm_ref, gpf_ref, gqf_ref,
                     wu_ref, cw_ref, cb_ref, wd_ref, y_ref, st_ref,
                     carry_ref, ext_ref, h2_ref, g_ref):
    t = pl.program_id(1)
    bs, bt, d = x_ref.shape
    m = bs * bt
    ck = FFN_CHUNK
    k1 = FFN_CONV_WIDTH - 1

    mix = (jnp.dot(oa_ref[...], wo_ref[0:D_ATT, :], preferred_element_type=F32)
           + jnp.dot(oc_ref[...], wo_ref[D_ATT:, :], preferred_element_type=F32))
    x1 = x_ref[...] + mod_ref[:, 2:3, :] * (_rms(mix.reshape(bs, bt, d)) * gpm_ref[...])
    y_ref[...] = x1
    h2 = _rms(x1) * gpf_ref[...] * (1.0 + mod_ref[:, 4:5, :]) + mod_ref[:, 3:4, :]
    h2_ref[...] = h2.reshape(m, d).astype(BF16)

    @pl.when(t == 0)
    def _():
        carry_ref[...] = jnp.zeros_like(carry_ref)
        carry_ref[:, :, SUBLANES - k1:SUBLANES, :] = hist_ref[...]

    def gate_chunk(c, slot):
        up = jnp.dot(h2_ref[...], wu_ref[c], preferred_element_type=F32).reshape(bs, bt, 2 * ck)
        ext = ext_ref.at[slot]
        ext[:, 0:SUBLANES, :] = carry_ref[c]
        ext[:, SUBLANES:, :] = up
        carry_ref[c] = up[:, bt - SUBLANES:, :]
        st_ref[c] = up[:, bt - k1:, :]
        w = cw_ref[c]
        u = (w[0:1, :] * ext[:, pl.ds(SUBLANES - 2, bt), :]
             + w[1:2, :] * ext[:, pl.ds(SUBLANES - 1, bt), :]
             + w[2:3, :] * up + cb_ref[c])
        ua = u[:, :, 0:ck]
        g_ref[c] = (ua * _sigmoid(ua) * u[:, :, ck:]).reshape(m, ck).astype(BF16)

    def body(i, _):
        gate_chunk(2 * i, 0)
        gate_chunk(2 * i + 1, 1)
        return 0

    lax.fori_loop(0, N_FFN_CHUNKS // 2, body, 0)
    for c in range(N_FFN_CHUNKS - N_FFN_CHUNKS % 2, N_FFN_CHUNKS):
        gate_chunk(c, 0)
    ffn = jnp.dot(g_ref[0], wd_ref[0], preferred_element_type=F32)
    for c in range(1, N_FFN_CHUNKS):
        ffn = ffn + jnp.dot(g_ref[c], wd_ref[c], preferred_element_type=F32)
    ffn = ffn.reshape(bs, bt, d)
    y_ref[...] = y_ref[...] + mod_ref[:, 5:6, :] * (_rms(ffn) * gqf_ref[...])


def _post_ffn(x3, o_att, o_conv, mod, hist_c, w_out, g_post_mix, g_pre_ffn, g_post_ffn,
              w_up_c, cw_c, cb_c, w_down_c, bs, bt):
    n_seq, t, d = x3.shape
    m = bs * bt
    nt = t // bt
    k1 = FFN_CONV_WIDTH - 1
    nc = N_FFN_CHUNKS
    ck2 = 2 * FFN_CHUNK
    row = lambda i, j: (i * nt + j, 0)
    return pl.pallas_call(
        _post_ffn_kernel,
        grid=(n_seq // bs, nt),
        in_specs=[pl.BlockSpec((bs, bt, d), lambda i, j: (i, j, 0)),
                  pl.BlockSpec((m, D_ATT), row),
                  pl.BlockSpec((m, D_CONV), row),
                  pl.BlockSpec((bs, N_MOD, d), lambda i, j: (i, 0, 0)),
                  pl.BlockSpec((nc, bs, k1, ck2), lambda i, j: (0, i, 0, 0)),
                  _const_spec((d, d)),
                  _const_spec((1, d)), _const_spec((1, d)), _const_spec((1, d)),
                  _const_spec((nc, d, ck2)),
                  _const_spec((nc, FFN_CONV_WIDTH, ck2)),
                  _const_spec((nc, 1, ck2)),
                  _const_spec((nc, FFN_CHUNK, d))],
        out_specs=[pl.BlockSpec((bs, bt, d), lambda i, j: (i, j, 0)),
                   pl.BlockSpec((nc, bs, k1, ck2), lambda i, j: (0, i, 0, 0))],
        out_shape=[jax.ShapeDtypeStruct((n_seq, t, d), F32),
                   jax.ShapeDtypeStruct((nc, n_seq, k1, ck2), F32)],
        scratch_shapes=[pltpu.VMEM((nc, bs, SUBLANES, ck2), F32),
                        pltpu.VMEM((2, bs, SUBLANES + bt, ck2), F32),
                        pltpu.VMEM((m, d), BF16),
                        pltpu.VMEM((nc, m, FFN_CHUNK), BF16)],
        compiler_params=_params(2),
        name="post_ffn",
    )(x3, o_att, o_conv, mod, hist_c, w_out, g_post_mix, g_pre_ffn, g_post_ffn,
      w_up_c, cw_c, cb_c, w_down_c)


def _chunk_cols(a):
    lead = a.shape[:-1]
    a = a.reshape(lead + (2, N_FFN_CHUNKS, FFN_CHUNK))
    a = jnp.moveaxis(a, -2, 0)
    return a.reshape((N_FFN_CHUNKS,) + lead + (2 * FFN_CHUNK,))


def _unchunk_cols(a):
    lead = a.shape[1:-1]
    a = a.reshape((N_FFN_CHUNKS,) + lead + (2, FFN_CHUNK))
    a = jnp.moveaxis(a, 0, -2)
    return a.reshape(lead + (2 * D_FF,))


PROMPT_TILES = dict(mix=(1, 512), conv=256, ffn=(1, 512), tq=512)
SAMPLE_TILES = dict(mix=(32, 8), conv=8, ffn=(16, 8), n_grp=16)


def _layer(x3, mod, conv_hist, ffn_hist, attend, tiles, lw):
    n_seq, t, d = x3.shape
    q, kb, vb, k, v, lf, z = _mix_in(x3, mod, lw["g_pre_mix"], lw["w_in_r"], lw["bf_pad"], *tiles["mix"])
    o_att = attend(q, kb, vb, k, v, lf)
    o_conv, conv_state = _conv(z.reshape(n_seq, t, D_CONV), conv_hist, lw["w_dw"], lw["b_dw"],
                               lw["ln_g"], lw["ln_b"], tiles["conv"])
    y, ffn_state_c = _post_ffn(x3, o_att, o_conv.reshape(n_seq * t, D_CONV), mod, _chunk_cols(ffn_hist),
                               lw["w_out"], lw["g_post_mix"], lw["g_pre_ffn"], lw["g_post_ffn"],
                               lw["w_up_c"], lw["cw_c"], lw["cb_c"], lw["w_down_c"], *tiles["ffn"])
    return (y, k.reshape(n_seq, t, N_HEADS, HEAD_DIM), v.reshape(n_seq, t, N_HEADS, HEAD_DIM),
            lf.reshape(n_seq, t, N_HEADS), conv_state, _unchunk_cols(ffn_state_c))


def kernel(x_prompt, x_sample, cache_k, cache_v, cache_logf, state_conv, state_ffn, page_table, c_prompt, c_sample, w_ada, b_ada, g_pre_mix, g_post_mix, g_pre_ffn, g_post_ffn, w_in, b_f, w_dw, b_dw, ln_g, ln_b, w_out, w_up, w_ffn_dw, b_ffn_dw, w_down):
    depth = w_ada.shape[0]
    b, t, d = x_prompt.shape
    s_, t_new, _ = x_sample.shape
    n_phys = cache_k.shape[1]
    yp, ys = x_prompt, x_sample
    outs_p, outs_s = [], []
    c_all = jnp.concatenate([c_prompt, c_sample], axis=0)
    c_all = jnp.pad(c_all, ((0, (-c_all.shape[0]) % SUBLANES), (0, 0)))
    for l in range(depth):
        row = lambda a: a[l].reshape(1, -1)
        wi = w_in[l]
        w_in_r = jnp.concatenate(
            [wi[:, :3 * D_ATT], wi[:, 3 * D_ATT + N_HEADS:],
             jnp.pad(wi[:, 3 * D_ATT:3 * D_ATT + N_HEADS], ((0, 0), (0, LANES - N_HEADS)))],
            axis=1).astype(BF16)
        lw = dict(
            g_pre_mix=row(g_pre_mix), g_post_mix=row(g_post_mix),
            g_pre_ffn=row(g_pre_ffn), g_post_ffn=row(g_post_ffn),
            w_in_r=w_in_r, bf_pad=jnp.pad(b_f[l], (0, LANES - N_HEADS)).reshape(1, LANES),
            w_dw=w_dw[l], b_dw=b_dw[l], ln_g=ln_g[l], ln_b=ln_b[l],
            w_out=w_out[l].astype(BF16),
            w_up_c=_chunk_cols(w_up[l].astype(BF16)),
            cw_c=_chunk_cols(w_ffn_dw[l]),
            cb_c=_chunk_cols(b_ffn_dw[l].reshape(1, -1)),
            w_down_c=w_down[l].astype(BF16).reshape(N_FFN_CHUNKS, FFN_CHUNK, d),
        )
        mod = _ada(c_all, w_ada[l], b_ada[l])
        mod_p = mod[:b].reshape(b, N_MOD, d)
        mod_s = mod[b:b + s_].reshape(s_, N_MOD, d)

        def attend_p(q, kb, vb, k, v, lf):
            cum = _cumsum_lanes(lf.reshape(b, t, N_HEADS).transpose(0, 2, 1))
            cum_col = cum.reshape(b, N_HEADS // 2, 2, t).transpose(0, 1, 3, 2)
            return _fox_prompt(q, kb, vb, cum_col, b, t, PROMPT_TILES["tq"])

        def attend_s(q, kb, vb, k, v, lf):
            cn = _cumsum_rows(lf.reshape(s_, t_new, N_HEADS))
            o_t = _fox_sample(q.reshape(s_, t_new, D_ATT),
                              k.reshape(s_, t_new, D_ATT).transpose(0, 2, 1),
                              v.reshape(s_, t_new, D_ATT).transpose(0, 2, 1),
                              cn.reshape(s_, t_new * N_HEADS, 1), cn.transpose(0, 2, 1),
                              cache_k.transpose(0, 1, 3, 4, 2), cache_v.transpose(0, 1, 3, 4, 2),
                              cache_logf.transpose(0, 1, 3, 2), page_table, l, SAMPLE_TILES["n_grp"])
            return o_t.transpose(0, 2, 1).reshape(s_ * t_new, D_ATT)

        conv0 = jnp.zeros((b, CONV_WIDTH - 1, D_CONV), F32)
        ffn0 = jnp.zeros((b, FFN_CONV_WIDTH - 1, 2 * D_FF), F32)
        yp, *rp = _layer(yp, mod_p, conv0, ffn0, attend_p, PROMPT_TILES, lw)
        ys, *rs = _layer(ys, mod_s, state_conv[l], state_ffn[l], attend_s, SAMPLE_TILES, lw)
        outs_p.append(rp)
        outs_s.append(rs)
    stack = lambda outs, i: jnp.stack([o[i] for o in outs])
    return (yp, ys, *[stack(outs_p, i) for i in range(5)], *[stack(outs_s, i) for i in range(5)])
```

```python
import functools

import jax
import jax.numpy as jnp
from jax import lax
from jax.experimental import pallas as pl
from jax.experimental.pallas import tpu as pltpu

F32 = jnp.float32
BF16 = jnp.bfloat16

D_MODEL = 1024
HEAD_DIM = 64
N_HEADS = 8
D_ATT = N_HEADS * HEAD_DIM
D_CONV = D_MODEL - D_ATT
CONV_WIDTH = 31
D_FF = 2816
FFN_CONV_WIDTH = 3
N_MOD = 6
EPS = 1e-6
PAGE_SIZE = 128

LANES = 128
SUBLANES = 8
FFN_CHUNK = 256
N_FFN_CHUNKS = D_FF // FFN_CHUNK
D_IN_PAD = 3 * D_ATT + 2 * D_CONV + LANES
VMEM_LIMIT = 52 * 1024 * 1024


def _sigmoid(x):
    return 1.0 / (1.0 + jnp.exp(-x))


def _rms(x):
    return x * lax.rsqrt(jnp.mean(x * x, axis=-1, keepdims=True) + EPS)


def _params(n_grid):
    return pltpu.CompilerParams(dimension_semantics=("arbitrary",) * n_grid,
                                vmem_limit_bytes=VMEM_LIMIT)


def _const_spec(shape):
    nd = len(shape)
    return pl.BlockSpec(shape, lambda *_: (0,) * nd, pipeline_mode=pl.Buffered(1))


def _ada_kernel(c_ref, w_ref, b_ref, o_ref):
    c = c_ref[...]
    s = (c * _sigmoid(c)).astype(BF16)
    o_ref[...] = jnp.dot(s, w_ref[...].astype(BF16), preferred_element_type=F32) + b_ref[...]


def _ada(c_all, w_ada, b_ada):
    r = c_all.shape[0]
    n = w_ada.shape[1]
    bn = D_MODEL
    return pl.pallas_call(
        _ada_kernel,
        grid=(n // bn,),
        in_specs=[pl.BlockSpec((r, D_MODEL), lambda j: (0, 0)),
                  pl.BlockSpec((D_MODEL, bn), lambda j: (0, j)),
                  pl.BlockSpec((1, bn), lambda j: (0, j))],
        out_specs=pl.BlockSpec((r, bn), lambda j: (0, j)),
        out_shape=jax.ShapeDtypeStruct((r, n), F32),
        compiler_params=_params(1),
        name="ada",
    )(c_all, w_ada, b_ada.reshape(1, n))


def _mix_in_kernel(x_ref, mod_ref, g_ref, w_ref, bf_ref,
                   q_ref, kb_ref, vb_ref, k_ref, v_ref, lf_ref, z_ref, *, time_minor):
    x = x_ref[...]
    bs, bt, d = x.shape
    h = _rms(x) * g_ref[...] * (1.0 + mod_ref[:, 1:2, :]) + mod_ref[:, 0:1, :]
    h = h.reshape(bs * bt, d).astype(BF16)
    p = jnp.dot(h, w_ref[...], preferred_element_type=F32)
    q = p[:, 0:D_ATT]
    k = p[:, D_ATT:2 * D_ATT]
    v = p[:, 2 * D_ATT:3 * D_ATT]
    a = p[:, 3 * D_ATT:3 * D_ATT + D_CONV]
    g = p[:, 3 * D_ATT + D_CONV:3 * D_ATT + 2 * D_CONV]
    f = p[:, 3 * D_ATT + 2 * D_CONV:] + bf_ref[...]
    q_ref[...] = (q * (HEAD_DIM ** -0.5)).astype(BF16)
    kb_ref[...] = k.astype(BF16)
    vb_ref[...] = v.astype(BF16)
    lf = jnp.minimum(f, 0.0) - jnp.log(1.0 + jnp.exp(-jnp.abs(f)))
    if time_minor:
        k_ref[...] = k.T
        v_ref[...] = v.T
        lf_ref[...] = lf.T[0:N_HEADS, :]
    else:
        k_ref[...] = k
        v_ref[...] = v
        lf_ref[...] = lf[:, 0:N_HEADS]
    z_ref[...] = a * _sigmoid(g)


def _mix_in(x3, mod, g_pre, w_in_r, bf_pad, bs, bt, time_minor):
    n_seq, t, d = x3.shape
    n_tok = n_seq * t
    m = bs * bt
    nt = t // bt
    row = lambda i, j: (i * nt + j, 0)
    tok = lambda c, dt: jax.ShapeDtypeStruct((n_tok, c), dt)
    if time_minor:
        assert bs == 1
        kv_spec = pl.BlockSpec((None, D_ATT, bt), lambda i, j: (i, 0, j))
        kv_shape = jax.ShapeDtypeStruct((n_seq, D_ATT, t), F32)
        lf_spec = pl.BlockSpec((None, N_HEADS, bt), lambda i, j: (i, 0, j))
        lf_shape = jax.ShapeDtypeStruct((n_seq, N_HEADS, t), F32)
    else:
        kv_spec, kv_shape = pl.BlockSpec((m, D_ATT), row), tok(D_ATT, F32)
        lf_spec, lf_shape = pl.BlockSpec((m, N_HEADS), row), tok(N_HEADS, F32)
    return pl.pallas_call(
        functools.partial(_mix_in_kernel, time_minor=time_minor),
        grid=(n_seq // bs, nt),
        in_specs=[pl.BlockSpec((bs, bt, d), lambda i, j: (i, j, 0)),
                  pl.BlockSpec((bs, N_MOD, d), lambda i, j: (i, 0, 0)),
                  _const_spec((1, d)),
                  _const_spec((d, D_IN_PAD)),
                  _const_spec((1, LANES))],
        out_specs=[pl.BlockSpec((m, D_ATT), row)] * 3
                  + [kv_spec, kv_spec, lf_spec, pl.BlockSpec((m, D_CONV), row)],
        out_shape=[tok(D_ATT, BF16), tok(D_ATT, BF16), tok(D_ATT, BF16),
                   kv_shape, kv_shape, lf_shape, tok(D_CONV, F32)],
        compiler_params=_params(2),
        name="mix_in",
    )(x3, mod, g_pre, w_in_r, bf_pad)


def _cumsum_lanes_kernel(x_ref, o_ref):
    x = x_ref[...]
    t = x.shape[-1]
    idx = lax.broadcasted_iota(jnp.int32, x.shape, 1)
    step = 1
    while step < t:
        x = x + jnp.where(idx >= step, pltpu.roll(x, step, axis=1), 0.0)
        step *= 2
    o_ref[...] = x


def _cumsum_lanes(x):
    b, h, t = x.shape
    return pl.pallas_call(
        _cumsum_lanes_kernel,
        grid=(b,),
        in_specs=[pl.BlockSpec((None, h, t), lambda i: (i, 0, 0))],
        out_specs=pl.BlockSpec((None, h, t), lambda i: (i, 0, 0)),
        out_shape=jax.ShapeDtypeStruct((b, h, t), F32),
        compiler_params=_params(1),
        name="cumsum_lanes",
    )(x)


def _cumsum_rows_kernel(x_ref, o_ref):
    x = x_ref[...]
    acc = x[:, 0:1, :]
    rows = [acc]
    for t in range(1, x.shape[1]):
        acc = acc + x[:, t:t + 1, :]
        rows.append(acc)
    o_ref[...] = jnp.concatenate(rows, axis=1)


def _cumsum_rows(x):
    return pl.pallas_call(
        _cumsum_rows_kernel,
        out_shape=jax.ShapeDtypeStruct(x.shape, F32),
        name="cumsum_rows",
    )(x)


def _split3(x):
    hi = x.astype(BF16).astype(F32)
    r = x - hi
    mid = r.astype(BF16).astype(F32)
    lo = (r - mid).astype(BF16).astype(F32)
    return hi, mid, lo


N_BIAS = 3


def _augment(x, own, lane, base, pieces, fill):
    out = jnp.where(own, x, 0.0)
    for i in range(N_BIAS):
        out = jnp.where(lane == base + i, pieces[i], out)
    return jnp.where((lane >= base + N_BIAS) & (lane < base + 2 * N_BIAS), fill, out)


def _fox_prompt_kernel(q_ref, k_ref, v_ref, cs_ref, ci_ref, o_ref,
                       kaug_ref, vt_ref, qt_ref, m_ref, acc_ref, ut_ref, *, tq):
    qi = pl.program_id(2)
    t = k_ref.shape[0]
    nk = t // tq
    lane = lax.broadcasted_iota(jnp.int32, (1, LANES), 1)
    sub_head = lax.broadcasted_iota(jnp.int32, (LANES, 1), 0) // HEAD_DIM

    @pl.when(qi == 0)
    def _():
        def prep(c, _):
            off = pl.multiple_of(c * tq, tq)
            kc = k_ref[pl.ds(off, tq), :].astype(F32)
            vt = v_ref[pl.ds(off, tq), :].astype(F32).T
            for h in range(2):
                own = (lane // HEAD_DIM) == h
                cs = jnp.broadcast_to(cs_ref[pl.ds(off, tq), h:h + 1], (tq, LANES))
                kaug = _augment(kc, own, lane, HEAD_DIM * (1 - h), _split3(cs), 1.0)
                kaug_ref[h, c] = kaug.astype(BF16)
                vt_ref[h, c] = jnp.where(sub_head == h, vt, 1.0).astype(BF16)
            return 0
        lax.fori_loop(0, nk, prep, 0)

    q = q_ref[...].astype(F32)
    for h in range(2):
        own = (lane // HEAD_DIM) == h
        ci = jnp.broadcast_to(ci_ref[:, h:h + 1], (tq, LANES))
        base = HEAD_DIM * (1 - h)
        qaug = _augment(q, own, lane, base + N_BIAS, _split3(ci), 0.0)
        qaug = jnp.where((lane >= base) & (lane < base + N_BIAS), -1.0, qaug)
        qt_ref[h] = qaug.T.astype(BF16)
    m_ref[...] = jnp.full(m_ref.shape, -jnp.inf, F32)
    acc_ref[...] = jnp.zeros_like(acc_ref)
    key_pos = lax.broadcasted_iota(jnp.int32, (tq, tq), 0)
    qry_pos = lax.broadcasted_iota(jnp.int32, (tq, tq), 1)

    def scores(kb):
        return [jnp.dot(kaug_ref[h, kb], qt_ref[h], preferred_element_type=F32) for h in range(2)]

    def block(kb, slot, masked, prefetch):
        uts = [ut_ref[slot, h] for h in range(2)]
        nxt = scores(kb + 1) if prefetch else None
        for h in range(2):
            ut = uts[h]
            if masked:
                ut = jnp.where(key_pos <= qry_pos, ut, -jnp.inf)
            m_old = m_ref[h]
            m_new = jnp.maximum(m_old, jnp.max(ut, axis=0, keepdims=True))
            p = jnp.exp(ut - m_new).astype(BF16)
            m_ref[h] = m_new
            acc_ref[h] = (jnp.exp(m_old - m_new) * acc_ref[h]
                          + jnp.dot(vt_ref[h, kb], p, preferred_element_type=F32))
        if prefetch:
            for h in range(2):
                ut_ref[1 - slot, h] = nxt[h]

    def body(i, _):
        block(2 * i, 0, False, True)
        block(2 * i + 1, 1, False, True)
        return 0

    first = scores(0)
    for h in range(2):
        ut_ref[0, h] = first[h]
    lax.fori_loop(0, qi // 2, body, 0)

    @pl.when(qi % 2 == 0)
    def _():
        block(qi, 0, True, False)

    @pl.when(qi % 2 == 1)
    def _():
        block(qi - 1, 0, False, True)
        block(qi, 1, True, False)

    a0 = acc_ref[0]
    a1 = acc_ref[1]
    ot = jnp.concatenate([a0[0:HEAD_DIM] / a0[HEAD_DIM:HEAD_DIM + 1],
                          a1[HEAD_DIM:] / a1[0:1]], axis=0)
    o_ref[...] = ot.T.astype(o_ref.dtype)


def _fox_prompt(q, kb, vb, cum_col, b, t, tq):
    nq = t // tq
    n_pairs = N_HEADS // 2
    return pl.pallas_call(
        functools.partial(_fox_prompt_kernel, tq=tq),
        grid=(b, n_pairs, nq),
        in_specs=[pl.BlockSpec((tq, LANES), lambda i, p, j: (i * nq + j, p)),
                  pl.BlockSpec((t, LANES), lambda i, p, j: (i, p)),
                  pl.BlockSpec((t, LANES), lambda i, p, j: (i, p)),
                  pl.BlockSpec((None, None, t, 2), lambda i, p, j: (i, p, 0, 0)),
                  pl.BlockSpec((None, None, tq, 2), lambda i, p, j: (i, p, j, 0))],
        out_specs=pl.BlockSpec((tq, LANES), lambda i, p, j: (i * nq + j, p)),
        out_shape=jax.ShapeDtypeStruct((b * t, D_ATT), BF16),
        scratch_shapes=[pltpu.VMEM((2, nq, tq, LANES), BF16),
                        pltpu.VMEM((2, nq, LANES, tq), BF16),
                        pltpu.VMEM((2, LANES, tq), BF16),
                        pltpu.VMEM((2, 1, tq), F32),
                        pltpu.VMEM((2, LANES, tq), F32),
                        pltpu.VMEM((2, 2, tq, tq), F32)],
        compiler_params=_params(3),
        name="fox_prompt",
    )(q, kb, vb, cum_col, cum_col)


def _fox_sample_kernel(pt_ref, q_ref, knt_ref, vnt_ref, cnr_ref, cnc_ref, *rest, n_grp):
    del pt_ref
    k_pages = rest[0:n_grp]
    v_pages = rest[n_grp:2 * n_grp]
    lf_pages = rest[2 * n_grp:3 * n_grp]
    o_ref = rest[3 * n_grp]
    qbd_ref, m_ref, l_ref, acc_ref, carry_ref = rest[3 * n_grp + 1:]
    j = pl.program_id(1)
    t_new = q_ref.shape[0]
    n_rows = t_new * N_HEADS
    n_keys = n_grp * PAGE_SIZE
    nt_dims = (((1,), (1,)), ((), ()))
    head_of_lane = lax.broadcasted_iota(jnp.int32, (N_HEADS, D_ATT), 1) // HEAD_DIM
    head_mask = (head_of_lane == lax.broadcasted_iota(jnp.int32, (N_HEADS, D_ATT), 0)).astype(F32)

    @pl.when(j == 0)
    def _():
        q = q_ref[...].astype(F32)
        qbd = (q[:, None, :] * head_mask[None]).reshape(n_rows, D_ATT)
        qbd_ref[...] = qbd.astype(BF16)
        s = jnp.dot(qbd, knt_ref[...], preferred_element_type=F32)
        u = (s.reshape(t_new, N_HEADS, t_new) - cnc_ref[...][None]).reshape(n_rows, t_new)
        u = u + cnr_ref[...]
        tok = lax.broadcasted_iota(jnp.int32, (n_rows, t_new), 0) // N_HEADS
        key = lax.broadcasted_iota(jnp.int32, (n_rows, t_new), 1)
        u = jnp.where(key <= tok, u, -jnp.inf)
        m = jnp.max(u, axis=1, keepdims=True)
        p = jnp.exp(u - m)
        m_ref[...] = m
        acc_ref[...] = lax.dot_general(vnt_ref[...], p, nt_dims, preferred_element_type=F32)
        l_ref[...] = lax.dot_general(jnp.ones((l_ref.shape[0], t_new), F32), p, nt_dims,
                                     preferred_element_type=F32)
        carry_ref[...] = jnp.zeros_like(carry_ref)

    src = lax.broadcasted_iota(jnp.int32, (PAGE_SIZE, 2 * PAGE_SIZE), 0)
    dst = lax.broadcasted_iota(jnp.int32, (PAGE_SIZE, 2 * PAGE_SIZE), 1)
    later = ((src > dst) | (dst >= PAGE_SIZE)).astype(F32)
    lf = jnp.concatenate([r[...] for r in lf_pages], axis=0)
    sums = jnp.dot(lf, later, precision=lax.Precision.HIGHEST, preferred_element_type=F32)
    run = carry_ref[...]
    rests = [None] * n_grp
    for g in reversed(range(n_grp)):
        rows = slice(g * N_HEADS, (g + 1) * N_HEADS)
        rests[g] = sums[rows, 0:PAGE_SIZE] + run
        run = run + sums[rows, PAGE_SIZE:]
    carry_ref[...] = run
    rest_all = jnp.concatenate(rests, axis=1)

    kt = jnp.concatenate([r[...].reshape(D_ATT, PAGE_SIZE).astype(BF16) for r in k_pages], axis=1)
    s = jnp.dot(qbd_ref[...], kt, preferred_element_type=F32)
    u = (s.reshape(t_new, N_HEADS, n_keys) + rest_all[None]).reshape(n_rows, n_keys)
    u = u + cnr_ref[...]
    m_old = m_ref[...]
    m_new = jnp.maximum(m_old, jnp.max(u, axis=1, keepdims=True))
    m_ref[...] = m_new
    p = jnp.exp(u - m_new).astype(BF16)
    eye = (lax.broadcasted_iota(jnp.int32, (n_rows, n_rows), 0)
           == lax.broadcasted_iota(jnp.int32, (n_rows, n_rows), 1))
    alpha = jnp.sum(jnp.where(eye, jnp.exp(m_old - m_new), 0.0), axis=0, keepdims=True)
    vt = jnp.concatenate([r[...].reshape(D_ATT, PAGE_SIZE).astype(BF16) for r in v_pages], axis=1)
    acc_ref[...] = alpha * acc_ref[...] + lax.dot_general(vt, p, nt_dims, preferred_element_type=F32)
    l_ref[...] = alpha * l_ref[...] + lax.dot_general(jnp.ones((l_ref.shape[0], n_keys), BF16), p,
                                                      nt_dims, preferred_element_type=F32)

    @pl.when(j == pl.num_programs(1) - 1)
    def _():
        o_t = acc_ref[...] / l_ref[0:1, :]
        keep = (lax.broadcasted_iota(jnp.int32, o_t.shape, 0) // HEAD_DIM
                == lax.broadcasted_iota(jnp.int32, o_t.shape, 1) % N_HEADS)
        gather = (lax.broadcasted_iota(jnp.int32, (n_rows, t_new), 0) // N_HEADS
                  == lax.broadcasted_iota(jnp.int32, (n_rows, t_new), 1)).astype(F32)
        o_ref[...] = jnp.dot(jnp.where(keep, o_t, 0.0), gather,
                             preferred_element_type=F32).astype(o_ref.dtype)


def _fox_sample(q3, knt, vnt, cn_rows, cn_cols, cache_kt, cache_vt, cache_lft, page_table,
                layer, n_grp):
    s_, t_new, _ = q3.shape
    n_rows = t_new * N_HEADS
    n_pages = page_table.shape[1]
    n_steps = n_pages // n_grp

    def page_of(s, j, pt, g):
        return pt[s * n_pages + (n_steps - 1 - j) * n_grp + g]

    def kv_map(g):
        return lambda s, j, pt: (layer, page_of(s, j, pt, g), 0, 0, 0)

    def lf_map(g):
        return lambda s, j, pt: (layer, page_of(s, j, pt, g), 0, 0)

    seq3 = lambda s, j, pt: (s, 0, 0)
    kv_block = (None, None, N_HEADS, HEAD_DIM, PAGE_SIZE)
    in_specs = [pl.BlockSpec((None, t_new, D_ATT), seq3),
                pl.BlockSpec((None, D_ATT, t_new), seq3),
                pl.BlockSpec((None, D_ATT, t_new), seq3),
                pl.BlockSpec((None, n_rows, 1), seq3),
                pl.BlockSpec((None, N_HEADS, t_new), seq3)]
    in_specs += [pl.BlockSpec(kv_block, kv_map(g)) for g in range(n_grp)]
    in_specs += [pl.BlockSpec(kv_block, kv_map(g)) for g in range(n_grp)]
    in_specs += [pl.BlockSpec((None, None, N_HEADS, PAGE_SIZE), lf_map(g)) for g in range(n_grp)]
    grid_spec = pltpu.PrefetchScalarGridSpec(
        num_scalar_prefetch=1,
        grid=(s_, n_steps),
        in_specs=in_specs,
        out_specs=pl.BlockSpec((None, D_ATT, t_new), seq3),
        scratch_shapes=[pltpu.VMEM((n_rows, D_ATT), BF16),
                        pltpu.VMEM((n_rows, 1), F32),
                        pltpu.VMEM((2 * SUBLANES, n_rows), F32),
                        pltpu.VMEM((D_ATT, n_rows), F32),
                        pltpu.VMEM((N_HEADS, PAGE_SIZE), F32)])
    return pl.pallas_call(
        functools.partial(_fox_sample_kernel, n_grp=n_grp),
        grid_spec=grid_spec,
        out_shape=jax.ShapeDtypeStruct((s_, D_ATT, t_new), BF16),
        compiler_params=_params(2),
        name="fox_sample",
    )(page_table.reshape(-1), q3, knt, vnt, cn_rows, cn_cols,
      *([cache_kt] * n_grp), *([cache_vt] * n_grp), *([cache_lft] * n_grp))


HALO = 32


def _conv_kernel(z_ref, hist_ref, w_ref, b_ref, g_ref, bb_ref, o_ref, st_ref, ext_ref, *, bt):
    t = pl.program_id(1)
    pad = HALO - (CONV_WIDTH - 1)

    bs = z_ref.shape[0]

    @pl.when(t == 0)
    def _():
        ext_ref[:, 0:pad, :] = jnp.zeros((bs, pad, D_CONV), F32)
        ext_ref[:, pad:HALO, :] = hist_ref[...]

    @pl.when(t > 0)
    def _():
        ext_ref[:, 0:HALO, :] = ext_ref[:, bt:bt + HALO, :]

    ext_ref[:, HALO:HALO + bt, :] = z_ref[...]
    acc = jnp.zeros((bs, bt, D_CONV), F32) + b_ref[...]
    for r in range(SUBLANES):
        rows = bt + (SUBLANES if r else 0)
        part = None
        for j in range(CONV_WIDTH):
            if (j + pad) % SUBLANES == r:
                term = w_ref[j:j + 1, :] * ext_ref[:, pl.ds(j + pad - r, rows), :]
                part = term if part is None else part + term
        acc = acc + part[:, r:r + bt, :]
    mu = jnp.mean(acc, axis=-1, keepdims=True)
    xc = acc - mu
    var = jnp.mean(xc * xc, axis=-1, keepdims=True)
    y = xc * lax.rsqrt(var + EPS) * g_ref[...] + bb_ref[...]
    o_ref[...] = (y * _sigmoid(y)).astype(o_ref.dtype)

    @pl.when(t == pl.num_programs(1) - 1)
    def _():
        st_ref[...] = ext_ref[:, bt + pad:bt + HALO, :]


def _conv(z3, hist, w_dw, b_dw, ln_g, ln_b, bs, bt):
    n_seq, t, c = z3.shape
    k1 = CONV_WIDTH - 1
    return pl.pallas_call(
        functools.partial(_conv_kernel, bt=bt),
        grid=(n_seq // bs, t // bt),
        in_specs=[pl.BlockSpec((bs, bt, c), lambda i, j: (i, j, 0)),
                  pl.BlockSpec((bs, k1, c), lambda i, j: (i, 0, 0)),
                  _const_spec((CONV_WIDTH, c)),
                  _const_spec((1, c)), _const_spec((1, c)), _const_spec((1, c))],
        out_specs=[pl.BlockSpec((bs, bt, c), lambda i, j: (i, j, 0)),
                   pl.BlockSpec((bs, k1, c), lambda i, j: (i, 0, 0))],
        out_shape=[jax.ShapeDtypeStruct((n_seq, t, c), BF16),
                   jax.ShapeDtypeStruct((n_seq, k1, c), F32)],
        scratch_shapes=[pltpu.VMEM((bs, HALO + bt, c), F32)],
        compiler_params=_params(2),
        name="conv",
    )(z3, hist, w_dw, b_dw.reshape(1, c), ln_g.reshape(1, c), ln_b.reshape(1, c))


def _post_ffn_kernel(x_ref, oa_ref, oc_ref, mod_ref, hist_ref, wo_ref, gpm_ref, gpf_ref, gqf_ref,
                     wu_ref, cw_ref, cb_ref, wd_ref, y_ref, st_ref,
                     carry_ref, ext_ref, h2_ref, g_ref):
    t = pl.program_id(1)
    bs, bt, d = x_ref.shape
    m = bs * bt
    ck = FFN_CHUNK
    k1 = FFN_CONV_WIDTH - 1

    mix = (jnp.dot(oa_ref[...], wo_ref[0:D_ATT, :], preferred_element_type=F32)
           + jnp.dot(oc_ref[...], wo_ref[D_ATT:, :], preferred_element_type=F32))
    x1 = x_ref[...] + mod_ref[:, 2:3, :] * (_rms(mix.reshape(bs, bt, d)) * gpm_ref[...])
    y_ref[...] = x1
    h2 = _rms(x1) * gpf_ref[...] * (1.0 + mod_ref[:, 4:5, :]) + mod_ref[:, 3:4, :]
    h2_ref[...] = h2.reshape(m, d).astype(BF16)

    @pl.when(t == 0)
    def _():
        carry_ref[...] = jnp.zeros_like(carry_ref)
        carry_ref[:, :, SUBLANES - k1:SUBLANES, :] = hist_ref[...]

    def gate_chunk(c, slot):
        up = jnp.dot(h2_ref[...], wu_ref[c], preferred_element_type=F32).reshape(bs, bt, 2 * ck)
        ext = ext_ref.at[slot]
        ext[:, 0:SUBLANES, :] = carry_ref[c]
        ext[:, SUBLANES:, :] = up
        carry_ref[c] = up[:, bt - SUBLANES:, :]
        st_ref[c] = up[:, bt - k1:, :]
        w = cw_ref[c]
        u = (w[0:1, :] * ext[:, pl.ds(SUBLANES - 2, bt), :]
             + w[1:2, :] * ext[:, pl.ds(SUBLANES - 1, bt), :]
             + w[2:3, :] * up + cb_ref[c])
        ua = u[:, :, 0:ck]
        g_ref[c] = (ua * _sigmoid(ua) * u[:, :, ck:]).reshape(m, ck).astype(BF16)

    def body(i, _):
        gate_chunk(2 * i, 0)
        gate_chunk(2 * i + 1, 1)
        return 0

    lax.fori_loop(0, N_FFN_CHUNKS // 2, body, 0)
    for c in range(N_FFN_CHUNKS - N_FFN_CHUNKS % 2, N_FFN_CHUNKS):
        gate_chunk(c, 0)
    ffn = jnp.dot(g_ref[0], wd_ref[0], preferred_element_type=F32)
    for c in range(1, N_FFN_CHUNKS):
        ffn = ffn + jnp.dot(g_ref[c], wd_ref[c], preferred_element_type=F32)
    ffn = ffn.reshape(bs, bt, d)
    y_ref[...] = y_ref[...] + mod_ref[:, 5:6, :] * (_rms(ffn) * gqf_ref[...])


def _post_ffn(x3, o_att, o_conv, mod, hist_c, w_out, g_post_mix, g_pre_ffn, g_post_ffn,
              w_up_c, cw_c, cb_c, w_down_c, bs, bt):
    n_seq, t, d = x3.shape
    m = bs * bt
    nt = t // bt
    k1 = FFN_CONV_WIDTH - 1
    nc = N_FFN_CHUNKS
    ck2 = 2 * FFN_CHUNK
    row = lambda i, j: (i * nt + j, 0)
    return pl.pallas_call(
        _post_ffn_kernel,
        grid=(n_seq // bs, nt),
        in_specs=[pl.BlockSpec((bs, bt, d), lambda i, j: (i, j, 0)),
                  pl.BlockSpec((m, D_ATT), row),
                  pl.BlockSpec((m, D_CONV), row),
                  pl.BlockSpec((bs, N_MOD, d), lambda i, j: (i, 0, 0)),
                  pl.BlockSpec((nc, bs, k1, ck2), lambda i, j: (0, i, 0, 0)),
                  _const_spec((d, d)),
                  _const_spec((1, d)), _const_spec((1, d)), _const_spec((1, d)),
                  _const_spec((nc, d, ck2)),
                  _const_spec((nc, FFN_CONV_WIDTH, ck2)),
                  _const_spec((nc, 1, ck2)),
                  _const_spec((nc, FFN_CHUNK, d))],
        out_specs=[pl.BlockSpec((bs, bt, d), lambda i, j: (i, j, 0)),
                   pl.BlockSpec((nc, bs, k1, ck2), lambda i, j: (0, i, 0, 0))],
        out_shape=[jax.ShapeDtypeStruct((n_seq, t, d), F32),
                   jax.ShapeDtypeStruct((nc, n_seq, k1, ck2), F32)],
        scratch_shapes=[pltpu.VMEM((nc, bs, SUBLANES, ck2), F32),
                        pltpu.VMEM((2, bs, SUBLANES + bt, ck2), F32),
                        pltpu.VMEM((m, d), BF16),
                        pltpu.VMEM((nc, m, FFN_CHUNK), BF16)],
        compiler_params=_params(2),
        name="post_ffn",
    )(x3, o_att, o_conv, mod, hist_c, w_out, g_post_mix, g_pre_ffn, g_post_ffn,
      w_up_c, cw_c, cb_c, w_down_c)


def _chunk_cols(a):
    lead = a.shape[:-1]
    a = a.reshape(lead + (2, N_FFN_CHUNKS, FFN_CHUNK))
    a = jnp.moveaxis(a, -2, 0)
    return a.reshape((N_FFN_CHUNKS,) + lead + (2 * FFN_CHUNK,))


def _unchunk_cols(a):
    lead = a.shape[1:-1]
    a = a.reshape((N_FFN_CHUNKS,) + lead + (2, FFN_CHUNK))
    a = jnp.moveaxis(a, 0, -2)
    return a.reshape(lead + (2 * D_FF,))


PROMPT_TILES = dict(mix=(1, 512, True), conv=(1, 256), ffn=(1, 512), tq=512)
SAMPLE_TILES = dict(mix=(32, 8, False), conv=(16, 8), ffn=(16, 8), n_grp=16)


def _layer(x3, mod, conv_hist, ffn_hist, attend, tiles, lw):
    n_seq, t, d = x3.shape
    q, kb, vb, k, v, lf, z = _mix_in(x3, mod, lw["g_pre_mix"], lw["w_in_r"], lw["bf_pad"], *tiles["mix"])
    o_att = attend(q, kb, vb, k, v, lf)
    o_conv, conv_state = _conv(z.reshape(n_seq, t, D_CONV), conv_hist, lw["w_dw"], lw["b_dw"],
                               lw["ln_g"], lw["ln_b"], *tiles["conv"])
    y, ffn_state_c = _post_ffn(x3, o_att, o_conv.reshape(n_seq * t, D_CONV), mod, _chunk_cols(ffn_hist),
                               lw["w_out"], lw["g_post_mix"], lw["g_pre_ffn"], lw["g_post_ffn"],
                               lw["w_up_c"], lw["cw_c"], lw["cb_c"], lw["w_down_c"], *tiles["ffn"])
    if tiles["mix"][2]:
        k4 = k.reshape(n_seq, N_HEADS, HEAD_DIM, t).transpose(0, 3, 1, 2)
        v4 = v.reshape(n_seq, N_HEADS, HEAD_DIM, t).transpose(0, 3, 1, 2)
        lf3 = lf.transpose(0, 2, 1)
    else:
        k4 = k.reshape(n_seq, t, N_HEADS, HEAD_DIM)
        v4 = v.reshape(n_seq, t, N_HEADS, HEAD_DIM)
        lf3 = lf.reshape(n_seq, t, N_HEADS)
    return y, k4, v4, lf3, conv_state, _unchunk_cols(ffn_state_c)


def kernel(x_prompt, x_sample, cache_k, cache_v, cache_logf, state_conv, state_ffn, page_table, c_prompt, c_sample, w_ada, b_ada, g_pre_mix, g_post_mix, g_pre_ffn, g_post_ffn, w_in, b_f, w_dw, b_dw, ln_g, ln_b, w_out, w_up, w_ffn_dw, b_ffn_dw, w_down):
    depth = w_ada.shape[0]
    b, t, d = x_prompt.shape
    s_, t_new, _ = x_sample.shape
    n_phys = cache_k.shape[1]
    yp, ys = x_prompt, x_sample
    outs_p, outs_s = [], []
    c_all = jnp.concatenate([c_prompt, c_sample], axis=0)
    c_all = jnp.pad(c_all, ((0, (-c_all.shape[0]) % SUBLANES), (0, 0)))
    for l in range(depth):
        row = lambda a: a[l].reshape(1, -1)
        wi = w_in[l]
        w_in_r = jnp.concatenate(
            [wi[:, :3 * D_ATT], wi[:, 3 * D_ATT + N_HEADS:],
             jnp.pad(wi[:, 3 * D_ATT:3 * D_ATT + N_HEADS], ((0, 0), (0, LANES - N_HEADS)))],
            axis=1).astype(BF16)
        lw = dict(
            g_pre_mix=row(g_pre_mix), g_post_mix=row(g_post_mix),
            g_pre_ffn=row(g_pre_ffn), g_post_ffn=row(g_post_ffn),
            w_in_r=w_in_r, bf_pad=jnp.pad(b_f[l], (0, LANES - N_HEADS)).reshape(1, LANES),
            w_dw=w_dw[l], b_dw=b_dw[l], ln_g=ln_g[l], ln_b=ln_b[l],
            w_out=w_out[l].astype(BF16),
            w_up_c=_chunk_cols(w_up[l].astype(BF16)),
            cw_c=_chunk_cols(w_ffn_dw[l]),
            cb_c=_chunk_cols(b_ffn_dw[l].reshape(1, -1)),
            w_down_c=w_down[l].astype(BF16).reshape(N_FFN_CHUNKS, FFN_CHUNK, d),
        )
        mod = _ada(c_all, w_ada[l], b_ada[l])
        mod_p = mod[:b].reshape(b, N_MOD, d)
        mod_s = mod[b:b + s_].reshape(s_, N_MOD, d)

        def attend_p(q, kb, vb, k, v, lf):
            cum = _cumsum_lanes(lf)
            cum_col = cum.reshape(b, N_HEADS // 2, 2, t).transpose(0, 1, 3, 2)
            return _fox_prompt(q, kb, vb, cum_col, b, t, PROMPT_TILES["tq"])

        def attend_s(q, kb, vb, k, v, lf):
            cn = _cumsum_rows(lf.reshape(s_, t_new, N_HEADS))
            o_t = _fox_sample(q.reshape(s_, t_new, D_ATT),
                              k.reshape(s_, t_new, D_ATT).transpose(0, 2, 1),
                              v.reshape(s_, t_new, D_ATT).transpose(0, 2, 1),
                              cn.reshape(s_, t_new * N_HEADS, 1), cn.transpose(0, 2, 1),
                              cache_k.transpose(0, 1, 3, 4, 2), cache_v.transpose(0, 1, 3, 4, 2),
                              cache_logf.transpose(0, 1, 3, 2), page_table, l, SAMPLE_TILES["n_grp"])
            return o_t.transpose(0, 2, 1).reshape(s_ * t_new, D_ATT)

        conv0 = jnp.zeros((b, CONV_WIDTH - 1, D_CONV), F32)
        ffn0 = jnp.zeros((b, FFN_CONV_WIDTH - 1, 2 * D_FF), F32)
        yp, *rp = _layer(yp, mod_p, conv0, ffn0, attend_p, PROMPT_TILES, lw)
        ys, *rs = _layer(ys, mod_s, state_conv[l], state_ffn[l], attend_s, SAMPLE_TILES, lw)
        outs_p.append(rp)
        outs_s.append(rs)
    stack = lambda outs, i: jnp.stack([o[i] for o in outs])
    return (yp, ys, *[stack(outs_p, i) for i in range(5)], *[stack(outs_s, i) for i in range(5)])
```

```python
import functools

import jax
import jax.numpy as jnp
from jax import lax
from jax.experimental import pallas as pl
from jax.experimental.pallas import tpu as pltpu

F32 = jnp.float32
BF16 = jnp.bfloat16

D_MODEL = 1024
HEAD_DIM = 64
N_HEADS = 8
D_ATT = N_HEADS * HEAD_DIM
D_CONV = D_MODEL - D_ATT
CONV_WIDTH = 31
D_FF = 2816
FFN_CONV_WIDTH = 3
N_MOD = 6
EPS = 1e-6
PAGE_SIZE = 128

LANES = 128
SUBLANES = 8
FFN_CHUNK = 256
N_FFN_CHUNKS = D_FF // FFN_CHUNK
D_IN_PAD = 3 * D_ATT + 2 * D_CONV + LANES
VMEM_LIMIT = 52 * 1024 * 1024


def _sigmoid(x):
    return 1.0 / (1.0 + jnp.exp(-x))


def _rms(x):
    return x * lax.rsqrt(jnp.mean(x * x, axis=-1, keepdims=True) + EPS)


def _params(n_grid):
    return pltpu.CompilerParams(dimension_semantics=("arbitrary",) * n_grid,
                                vmem_limit_bytes=VMEM_LIMIT)


def _const_spec(shape):
    nd = len(shape)
    return pl.BlockSpec(shape, lambda *_: (0,) * nd, pipeline_mode=pl.Buffered(1))


def _ada_kernel(c_ref, w_ref, b_ref, o_ref):
    c = c_ref[...]
    s = (c * _sigmoid(c)).astype(BF16)
    o_ref[...] = jnp.dot(s, w_ref[...].astype(BF16), preferred_element_type=F32) + b_ref[...]


def _ada(c_all, w_ada, b_ada):
    r = c_all.shape[0]
    n = w_ada.shape[1]
    bn = D_MODEL
    return pl.pallas_call(
        _ada_kernel,
        grid=(n // bn,),
        in_specs=[pl.BlockSpec((r, D_MODEL), lambda j: (0, 0)),
                  pl.BlockSpec((D_MODEL, bn), lambda j: (0, j)),
                  pl.BlockSpec((1, bn), lambda j: (0, j))],
        out_specs=pl.BlockSpec((r, bn), lambda j: (0, j)),
        out_shape=jax.ShapeDtypeStruct((r, n), F32),
        compiler_params=_params(1),
        name="ada",
    )(c_all, w_ada, b_ada.reshape(1, n))


def _mix_in_kernel(x_ref, mod_ref, g_ref, w_ref, bf_ref,
                   q_ref, kb_ref, vb_ref, k_ref, v_ref, lf_ref, z_ref, *, time_minor):
    x = x_ref[...]
    bs, bt, d = x.shape
    h = _rms(x) * g_ref[...] * (1.0 + mod_ref[:, 1:2, :]) + mod_ref[:, 0:1, :]
    h = h.reshape(bs * bt, d).astype(BF16)
    p = jnp.dot(h, w_ref[...], preferred_element_type=F32)
    q = p[:, 0:D_ATT]
    k = p[:, D_ATT:2 * D_ATT]
    v = p[:, 2 * D_ATT:3 * D_ATT]
    a = p[:, 3 * D_ATT:3 * D_ATT + D_CONV]
    g = p[:, 3 * D_ATT + D_CONV:3 * D_ATT + 2 * D_CONV]
    f = p[:, 3 * D_ATT + 2 * D_CONV:] + bf_ref[...]
    q_ref[...] = (q * (HEAD_DIM ** -0.5)).astype(BF16)
    kb_ref[...] = k.astype(BF16)
    vb_ref[...] = v.astype(BF16)
    lf = jnp.minimum(f, 0.0) - jnp.log(1.0 + jnp.exp(-jnp.abs(f)))
    if time_minor:
        k_ref[...] = k.T
        v_ref[...] = v.T
        lf_ref[...] = lf.T[0:N_HEADS, :]
    else:
        k_ref[...] = k
        v_ref[...] = v
        lf_ref[...] = lf[:, 0:N_HEADS]
    z_ref[...] = a * _sigmoid(g)


def _mix_in(x3, mod, g_pre, w_in_r, bf_pad, bs, bt, time_minor):
    n_seq, t, d = x3.shape
    n_tok = n_seq * t
    m = bs * bt
    nt = t // bt
    row = lambda i, j: (i * nt + j, 0)
    tok = lambda c, dt: jax.ShapeDtypeStruct((n_tok, c), dt)
    if time_minor:
        assert bs == 1
        kv_spec = pl.BlockSpec((None, D_ATT, bt), lambda i, j: (i, 0, j))
        kv_shape = jax.ShapeDtypeStruct((n_seq, D_ATT, t), F32)
        lf_spec = pl.BlockSpec((None, N_HEADS, bt), lambda i, j: (i, 0, j))
        lf_shape = jax.ShapeDtypeStruct((n_seq, N_HEADS, t), F32)
    else:
        kv_spec, kv_shape = pl.BlockSpec((m, D_ATT), row), tok(D_ATT, F32)
        lf_spec, lf_shape = pl.BlockSpec((m, N_HEADS), row), tok(N_HEADS, F32)
    return pl.pallas_call(
        functools.partial(_mix_in_kernel, time_minor=time_minor),
        grid=(n_seq // bs, nt),
        in_specs=[pl.BlockSpec((bs, bt, d), lambda i, j: (i, j, 0)),
                  pl.BlockSpec((bs, N_MOD, d), lambda i, j: (i, 0, 0)),
                  _const_spec((1, d)),
                  _const_spec((d, D_IN_PAD)),
                  _const_spec((1, LANES))],
        out_specs=[pl.BlockSpec((m, D_ATT), row)] * 3
                  + [kv_spec, kv_spec, lf_spec, pl.BlockSpec((m, D_CONV), row)],
        out_shape=[tok(D_ATT, BF16), tok(D_ATT, BF16), tok(D_ATT, BF16),
                   kv_shape, kv_shape, lf_shape, tok(D_CONV, F32)],
        compiler_params=_params(2),
        name="mix_in",
    )(x3, mod, g_pre, w_in_r, bf_pad)


def _cumsum_lanes_kernel(x_ref, o_ref):
    x = x_ref[...]
    t = x.shape[-1]
    idx = lax.broadcasted_iota(jnp.int32, x.shape, 1)
    step = 1
    while step < t:
        x = x + jnp.where(idx >= step, pltpu.roll(x, step, axis=1), 0.0)
        step *= 2
    o_ref[...] = x


def _cumsum_lanes(x):
    b, h, t = x.shape
    return pl.pallas_call(
        _cumsum_lanes_kernel,
        grid=(b,),
        in_specs=[pl.BlockSpec((None, h, t), lambda i: (i, 0, 0))],
        out_specs=pl.BlockSpec((None, h, t), lambda i: (i, 0, 0)),
        out_shape=jax.ShapeDtypeStruct((b, h, t), F32),
        compiler_params=_params(1),
        name="cumsum_lanes",
    )(x)


def _cumsum_rows_kernel(x_ref, o_ref):
    x = x_ref[...]
    acc = x[:, 0:1, :]
    rows = [acc]
    for t in range(1, x.shape[1]):
        acc = acc + x[:, t:t + 1, :]
        rows.append(acc)
    o_ref[...] = jnp.concatenate(rows, axis=1)


def _cumsum_rows(x):
    return pl.pallas_call(
        _cumsum_rows_kernel,
        out_shape=jax.ShapeDtypeStruct(x.shape, F32),
        name="cumsum_rows",
    )(x)


def _split3(x):
    hi = x.astype(BF16).astype(F32)
    r = x - hi
    mid = r.astype(BF16).astype(F32)
    lo = (r - mid).astype(BF16).astype(F32)
    return hi, mid, lo


N_BIAS = 3


def _augment(x, own, lane, base, pieces, fill):
    out = jnp.where(own, x, 0.0)
    for i in range(N_BIAS):
        out = jnp.where(lane == base + i, pieces[i], out)
    return jnp.where((lane >= base + N_BIAS) & (lane < base + 2 * N_BIAS), fill, out)


def _fox_prompt_kernel(q_ref, k_ref, v_ref, cs_ref, ci_ref, o_ref,
                       kaug_ref, vt_ref, qt_ref, m_ref, acc_ref, ut_ref, *, tq):
    qi = pl.program_id(2)
    t = k_ref.shape[0]
    nk = t // tq
    lane = lax.broadcasted_iota(jnp.int32, (1, LANES), 1)
    sub_head = lax.broadcasted_iota(jnp.int32, (LANES, 1), 0) // HEAD_DIM

    @pl.when(qi == 0)
    def _():
        def prep(c, _):
            off = pl.multiple_of(c * tq, tq)
            kc = k_ref[pl.ds(off, tq), :].astype(F32)
            vt = v_ref[pl.ds(off, tq), :].astype(F32).T
            for h in range(2):
                own = (lane // HEAD_DIM) == h
                cs = jnp.broadcast_to(cs_ref[pl.ds(off, tq), h:h + 1], (tq, LANES))
                kaug = _augment(kc, own, lane, HEAD_DIM * (1 - h), _split3(cs), 1.0)
                kaug_ref[h, c] = kaug.astype(BF16)
                vt_ref[h, c] = jnp.where(sub_head == h, vt, 1.0).astype(BF16)
            return 0
        lax.fori_loop(0, nk, prep, 0)

    q = q_ref[...].astype(F32)
    for h in range(2):
        own = (lane // HEAD_DIM) == h
        ci = jnp.broadcast_to(ci_ref[:, h:h + 1], (tq, LANES))
        base = HEAD_DIM * (1 - h)
        qaug = _augment(q, own, lane, base + N_BIAS, _split3(ci), 0.0)
        qaug = jnp.where((lane >= base) & (lane < base + N_BIAS), -1.0, qaug)
        qt_ref[h] = qaug.T.astype(BF16)
    m_ref[...] = jnp.full(m_ref.shape, -jnp.inf, F32)
    acc_ref[...] = jnp.zeros_like(acc_ref)
    key_pos = lax.broadcasted_iota(jnp.int32, (tq, tq), 0)
    qry_pos = lax.broadcasted_iota(jnp.int32, (tq, tq), 1)

    def scores(kb):
        return [jnp.dot(kaug_ref[h, kb], qt_ref[h], preferred_element_type=F32) for h in range(2)]

    def block(kb, slot, masked, prefetch):
        uts = [ut_ref[slot, h] for h in range(2)]
        nxt = scores(kb + 1) if prefetch else None
        for h in range(2):
            ut = uts[h]
            if masked:
                ut = jnp.where(key_pos <= qry_pos, ut, -jnp.inf)
            m_old = m_ref[h]
            m_new = jnp.maximum(m_old, jnp.max(ut, axis=0, keepdims=True))
            p = jnp.exp(ut - m_new).astype(BF16)
            m_ref[h] = m_new
            acc_ref[h] = (jnp.exp(m_old - m_new) * acc_ref[h]
                          + jnp.dot(vt_ref[h, kb], p, preferred_element_type=F32))
        if prefetch:
            for h in range(2):
                ut_ref[1 - slot, h] = nxt[h]

    def body(i, _):
        block(2 * i, 0, False, True)
        block(2 * i + 1, 1, False, True)
        return 0

    first = scores(0)
    for h in range(2):
        ut_ref[0, h] = first[h]
    lax.fori_loop(0, qi // 2, body, 0)

    @pl.when(qi % 2 == 0)
    def _():
        block(qi, 0, True, False)

    @pl.when(qi % 2 == 1)
    def _():
        block(qi - 1, 0, False, True)
        block(qi, 1, True, False)

    a0 = acc_ref[0]
    a1 = acc_ref[1]
    ot = jnp.concatenate([a0[0:HEAD_DIM] / a0[HEAD_DIM:HEAD_DIM + 1],
                          a1[HEAD_DIM:] / a1[0:1]], axis=0)
    o_ref[...] = ot.T.astype(o_ref.dtype)


def _fox_prompt(q, kb, vb, cum_col, b, t, tq):
    nq = t // tq
    n_pairs = N_HEADS // 2
    return pl.pallas_call(
        functools.partial(_fox_prompt_kernel, tq=tq),
        grid=(b, n_pairs, nq),
        in_specs=[pl.BlockSpec((tq, LANES), lambda i, p, j: (i * nq + j, p)),
                  pl.BlockSpec((t, LANES), lambda i, p, j: (i, p)),
                  pl.BlockSpec((t, LANES), lambda i, p, j: (i, p)),
                  pl.BlockSpec((None, None, t, 2), lambda i, p, j: (i, p, 0, 0)),
                  pl.BlockSpec((None, None, tq, 2), lambda i, p, j: (i, p, j, 0))],
        out_specs=pl.BlockSpec((tq, LANES), lambda i, p, j: (i * nq + j, p)),
        out_shape=jax.ShapeDtypeStruct((b * t, D_ATT), BF16),
        scratch_shapes=[pltpu.VMEM((2, nq, tq, LANES), BF16),
                        pltpu.VMEM((2, nq, LANES, tq), BF16),
                        pltpu.VMEM((2, LANES, tq), BF16),
                        pltpu.VMEM((2, 1, tq), F32),
                        pltpu.VMEM((2, LANES, tq), F32),
                        pltpu.VMEM((2, 2, tq, tq), F32)],
        compiler_params=_params(3),
        name="fox_prompt",
    )(q, kb, vb, cum_col, cum_col)


def _fox_sample_kernel(pt_ref, q_ref, knt_ref, vnt_ref, cnr_ref, cnc_ref, k_hbm, v_hbm, lf_hbm,
                       o_ref, qbd_ref, m_ref, l_ref, acc_ref, carry_ref, kbuf_ref, vbuf_ref,
                       lfbuf_ref, sem, *, n_grp, n_pages, layer):
    s_id = pl.program_id(0)
    j = pl.program_id(1)
    n_steps = pl.num_programs(1)
    step = s_id * n_steps + j
    slot = step % 2

    def page_copies(seq, jj, dst_slot):
        first = seq * n_pages + (n_steps - 1 - jj) * n_grp
        lf_copies, kv_copies = [], []
        for g in range(n_grp):
            page = pt_ref[first + g]
            lf_copies.append(pltpu.make_async_copy(lf_hbm.at[layer, page], lfbuf_ref.at[dst_slot, g],
                                                   sem.at[2, dst_slot]))
            kv_copies.append(pltpu.make_async_copy(k_hbm.at[layer, page], kbuf_ref.at[dst_slot, g],
                                                   sem.at[0, dst_slot]))
            kv_copies.append(pltpu.make_async_copy(v_hbm.at[layer, page], vbuf_ref.at[dst_slot, g],
                                                   sem.at[1, dst_slot]))
        return lf_copies, kv_copies

    def start_all(seq, jj, dst_slot):
        lf_copies, kv_copies = page_copies(seq, jj, dst_slot)
        for c in lf_copies + kv_copies:
            c.start()

    @pl.when(step == 0)
    def _():
        start_all(s_id, j, slot)

    @pl.when(step + 1 < pl.num_programs(0) * n_steps)
    def _():
        wrap = j + 1 == n_steps
        start_all(jnp.where(wrap, s_id + 1, s_id), jnp.where(wrap, 0, j + 1), 1 - slot)

    lf_waits, kv_waits = page_copies(s_id, j, slot)
    k_pages = [kbuf_ref.at[slot, g] for g in range(n_grp)]
    v_pages = [vbuf_ref.at[slot, g] for g in range(n_grp)]
    t_new = q_ref.shape[0]
    n_rows = t_new * N_HEADS
    n_keys = n_grp * PAGE_SIZE
    nt_dims = (((1,), (1,)), ((), ()))
    head_of_lane = lax.broadcasted_iota(jnp.int32, (N_HEADS, D_ATT), 1) // HEAD_DIM
    head_mask = (head_of_lane == lax.broadcasted_iota(jnp.int32, (N_HEADS, D_ATT), 0)).astype(F32)

    @pl.when(j == 0)
    def _():
        q = q_ref[...].astype(F32)
        qbd = (q[:, None, :] * head_mask[None]).reshape(n_rows, D_ATT)
        qbd_ref[...] = qbd.astype(BF16)
        s = jnp.dot(qbd, knt_ref[...], preferred_element_type=F32)
        u = (s.reshape(t_new, N_HEADS, t_new) - cnc_ref[...][None]).reshape(n_rows, t_new)
        u = u + cnr_ref[...]
        tok = lax.broadcasted_iota(jnp.int32, (n_rows, t_new), 0) // N_HEADS
        key = lax.broadcasted_iota(jnp.int32, (n_rows, t_new), 1)
        u = jnp.where(key <= tok, u, -jnp.inf)
        m = jnp.max(u, axis=1, keepdims=True)
        p = jnp.exp(u - m)
        m_ref[...] = m
        acc_ref[...] = lax.dot_general(vnt_ref[...], p, nt_dims, preferred_element_type=F32)
        l_ref[...] = lax.dot_general(jnp.ones((l_ref.shape[0], t_new), F32), p, nt_dims,
                                     preferred_element_type=F32)
        carry_ref[...] = jnp.zeros_like(carry_ref)

    src = lax.broadcasted_iota(jnp.int32, (PAGE_SIZE, 2 * PAGE_SIZE), 0)
    dst = lax.broadcasted_iota(jnp.int32, (PAGE_SIZE, 2 * PAGE_SIZE), 1)
    later = ((src > dst) | (dst >= PAGE_SIZE)).astype(F32)
    for c in lf_waits:
        c.wait()
    lf = lfbuf_ref[slot].reshape(n_grp * N_HEADS, PAGE_SIZE)
    sums = jnp.dot(lf, later, precision=lax.Precision.HIGHEST, preferred_element_type=F32)
    run = carry_ref[...]
    rests = [None] * n_grp
    for g in reversed(range(n_grp)):
        rows = slice(g * N_HEADS, (g + 1) * N_HEADS)
        rests[g] = sums[rows, 0:PAGE_SIZE] + run
        run = run + sums[rows, PAGE_SIZE:]
    carry_ref[...] = run
    rest_all = jnp.concatenate(rests, axis=1)

    for c in kv_waits:
        c.wait()
    kt = jnp.concatenate([r[...].reshape(D_ATT, PAGE_SIZE).astype(BF16) for r in k_pages], axis=1)
    s = jnp.dot(qbd_ref[...], kt, preferred_element_type=F32)
    u = (s.reshape(t_new, N_HEADS, n_keys) + rest_all[None]).reshape(n_rows, n_keys)
    u = u + cnr_ref[...]
    m_old = m_ref[...]
    m_new = jnp.maximum(m_old, jnp.max(u, axis=1, keepdims=True))
    m_ref[...] = m_new
    p = jnp.exp(u - m_new).astype(BF16)
    eye = (lax.broadcasted_iota(jnp.int32, (n_rows, n_rows), 0)
           == lax.broadcasted_iota(jnp.int32, (n_rows, n_rows), 1))
    alpha = jnp.sum(jnp.where(eye, jnp.exp(m_old - m_new), 0.0), axis=0, keepdims=True)
    vt = jnp.concatenate([r[...].reshape(D_ATT, PAGE_SIZE).astype(BF16) for r in v_pages], axis=1)
    acc_ref[...] = alpha * acc_ref[...] + lax.dot_general(vt, p, nt_dims, preferred_element_type=F32)
    l_ref[...] = alpha * l_ref[...] + lax.dot_general(jnp.ones((l_ref.shape[0], n_keys), BF16), p,
                                                      nt_dims, preferred_element_type=F32)

    @pl.when(j == pl.num_programs(1) - 1)
    def _():
        o_t = acc_ref[...] / l_ref[0:1, :]
        keep = (lax.broadcasted_iota(jnp.int32, o_t.shape, 0) // HEAD_DIM
                == lax.broadcasted_iota(jnp.int32, o_t.shape, 1) % N_HEADS)
        gather = (lax.broadcasted_iota(jnp.int32, (n_rows, t_new), 0) // N_HEADS
                  == lax.broadcasted_iota(jnp.int32, (n_rows, t_new), 1)).astype(F32)
        o_ref[...] = jnp.dot(jnp.where(keep, o_t, 0.0), gather,
                             preferred_element_type=F32).astype(o_ref.dtype)


def _fox_sample(q3, knt, vnt, cn_rows, cn_cols, cache_kt, cache_vt, cache_lft, page_table,
                layer, n_grp):
    s_, t_new, _ = q3.shape
    n_rows = t_new * N_HEADS
    n_pages = page_table.shape[1]
    n_steps = n_pages // n_grp

    seq3 = lambda s, j, pt: (s, 0, 0)
    in_specs = [pl.BlockSpec((None, t_new, D_ATT), seq3),
                pl.BlockSpec((None, D_ATT, t_new), seq3),
                pl.BlockSpec((None, D_ATT, t_new), seq3),
                pl.BlockSpec((None, n_rows, 1), seq3),
                pl.BlockSpec((None, N_HEADS, t_new), seq3),
                pl.BlockSpec(memory_space=pl.ANY),
                pl.BlockSpec(memory_space=pl.ANY),
                pl.BlockSpec(memory_space=pl.ANY)]
    page_buf = pltpu.VMEM((2, n_grp, N_HEADS, HEAD_DIM, PAGE_SIZE), F32)
    grid_spec = pltpu.PrefetchScalarGridSpec(
        num_scalar_prefetch=1,
        grid=(s_, n_steps),
        in_specs=in_specs,
        out_specs=pl.BlockSpec((None, D_ATT, t_new), seq3),
        scratch_shapes=[pltpu.VMEM((n_rows, D_ATT), BF16),
                        pltpu.VMEM((n_rows, 1), F32),
                        pltpu.VMEM((2 * SUBLANES, n_rows), F32),
                        pltpu.VMEM((D_ATT, n_rows), F32),
                        pltpu.VMEM((N_HEADS, PAGE_SIZE), F32),
                        page_buf, page_buf,
                        pltpu.VMEM((2, n_grp, N_HEADS, PAGE_SIZE), F32),
                        pltpu.SemaphoreType.DMA((3, 2))])
    return pl.pallas_call(
        functools.partial(_fox_sample_kernel, n_grp=n_grp, n_pages=n_pages, layer=layer),
        grid_spec=grid_spec,
        out_shape=jax.ShapeDtypeStruct((s_, D_ATT, t_new), BF16),
        compiler_params=_params(2),
        name="fox_sample",
    )(page_table.reshape(-1), q3, knt, vnt, cn_rows, cn_cols, cache_kt, cache_vt, cache_lft)


HALO = 32


def _conv_kernel(z_ref, hist_ref, w_ref, b_ref, g_ref, bb_ref, o_ref, st_ref, ext_ref, *, bt):
    t = pl.program_id(1)
    pad = HALO - (CONV_WIDTH - 1)

    bs = z_ref.shape[0]

    @pl.when(t == 0)
    def _():
        ext_ref[:, 0:pad, :] = jnp.zeros((bs, pad, D_CONV), F32)
        ext_ref[:, pad:HALO, :] = hist_ref[...]

    @pl.when(t > 0)
    def _():
        ext_ref[:, 0:HALO, :] = ext_ref[:, bt:bt + HALO, :]

    ext_ref[:, HALO:HALO + bt, :] = z_ref[...]
    acc = jnp.zeros((bs, bt, D_CONV), F32) + b_ref[...]
    for r in range(SUBLANES):
        rows = bt + (SUBLANES if r else 0)
        part = None
        for j in range(CONV_WIDTH):
            if (j + pad) % SUBLANES == r:
                term = w_ref[j:j + 1, :] * ext_ref[:, pl.ds(j + pad - r, rows), :]
                part = term if part is None else part + term
        acc = acc + part[:, r:r + bt, :]
    mu = jnp.mean(acc, axis=-1, keepdims=True)
    xc = acc - mu
    var = jnp.mean(xc * xc, axis=-1, keepdims=True)
    y = xc * lax.rsqrt(var + EPS) * g_ref[...] + bb_ref[...]
    o_ref[...] = (y * _sigmoid(y)).astype(o_ref.dtype)

    @pl.when(t == pl.num_programs(1) - 1)
    def _():
        st_ref[...] = ext_ref[:, bt + pad:bt + HALO, :]


def _conv(z3, hist, w_dw, b_dw, ln_g, ln_b, bs, bt):
    n_seq, t, c = z3.shape
    k1 = CONV_WIDTH - 1
    return pl.pallas_call(
        functools.partial(_conv_kernel, bt=bt),
        grid=(n_seq // bs, t // bt),
        in_specs=[pl.BlockSpec((bs, bt, c), lambda i, j: (i, j, 0)),
                  pl.BlockSpec((bs, k1, c), lambda i, j: (i, 0, 0)),
                  _const_spec((CONV_WIDTH, c)),
                  _const_spec((1, c)), _const_spec((1, c)), _const_spec((1, c))],
        out_specs=[pl.BlockSpec((bs, bt, c), lambda i, j: (i, j, 0)),
                   pl.BlockSpec((bs, k1, c), lambda i, j: (i, 0, 0))],
        out_shape=[jax.ShapeDtypeStruct((n_seq, t, c), BF16),
                   jax.ShapeDtypeStruct((n_seq, k1, c), F32)],
        scratch_shapes=[pltpu.VMEM((bs, HALO + bt, c), F32)],
        compiler_params=_params(2),
        name="conv",
    )(z3, hist, w_dw, b_dw.reshape(1, c), ln_g.reshape(1, c), ln_b.reshape(1, c))


def _post_ffn_kernel(x_ref, oa_ref, oc_ref, mod_ref, hist_ref, wo_ref, gpm_ref, gpf_ref, gqf_ref,
                     wu_ref, cw_ref, cb_ref, wd_ref, y_ref, st_ref,
                     carry_ref, ext_ref, h2_ref, g_ref):
    t = pl.program_id(1)
    bs, bt, d = x_ref.shape
    m = bs * bt
    ck = FFN_CHUNK
    k1 = FFN_CONV_WIDTH - 1

    mix = (jnp.dot(oa_ref[...], wo_ref[0:D_ATT, :], preferred_element_type=F32)
           + jnp.dot(oc_ref[...], wo_ref[D_ATT:, :], preferred_element_type=F32))
    x1 = x_ref[...] + mod_ref[:, 2:3, :] * (_rms(mix.reshape(bs, bt, d)) * gpm_ref[...])
    y_ref[...] = x1
    h2 = _rms(x1) * gpf_ref[...] * (1.0 + mod_ref[:, 4:5, :]) + mod_ref[:, 3:4, :]
    h2_ref[...] = h2.reshape(m, d).astype(BF16)

    @pl.when(t == 0)
    def _():
        carry_ref[...] = jnp.zeros_like(carry_ref)
        carry_ref[:, :, SUBLANES - k1:SUBLANES, :] = hist_ref[...]

    def gate_chunk(c, slot):
        up = jnp.dot(h2_ref[...], wu_ref[c], preferred_element_type=F32).reshape(bs, bt, 2 * ck)
        ext = ext_ref.at[slot]
        ext[:, 0:SUBLANES, :] = carry_ref[c]
        ext[:, SUBLANES:, :] = up
        carry_ref[c] = up[:, bt - SUBLANES:, :]
        st_ref[c] = up[:, bt - k1:, :]
        w = cw_ref[c]
        u = (w[0:1, :] * ext[:, pl.ds(SUBLANES - 2, bt), :]
             + w[1:2, :] * ext[:, pl.ds(SUBLANES - 1, bt), :]
             + w[2:3, :] * up + cb_ref[c])
        ua = u[:, :, 0:ck]
        g_ref[c] = (ua * _sigmoid(ua) * u[:, :, ck:]).reshape(m, ck).astype(BF16)

    def body(i, _):
        gate_chunk(2 * i, 0)
        gate_chunk(2 * i + 1, 1)
        return 0

    lax.fori_loop(0, N_FFN_CHUNKS // 2, body, 0)
    for c in range(N_FFN_CHUNKS - N_FFN_CHUNKS % 2, N_FFN_CHUNKS):
        gate_chunk(c, 0)
    ffn = jnp.dot(g_ref[0], wd_ref[0], preferred_element_type=F32)
    for c in range(1, N_FFN_CHUNKS):
        ffn = ffn + jnp.dot(g_ref[c], wd_ref[c], preferred_element_type=F32)
    ffn = ffn.reshape(bs, bt, d)
    y_ref[...] = y_ref[...] + mod_ref[:, 5:6, :] * (_rms(ffn) * gqf_ref[...])


def _post_ffn(x3, o_att, o_conv, mod, hist_c, w_out, g_post_mix, g_pre_ffn, g_post_ffn,
              w_up_c, cw_c, cb_c, w_down_c, bs, bt):
    n_seq, t, d = x3.shape
    m = bs * bt
    nt = t // bt
    k1 = FFN_CONV_WIDTH - 1
    nc = N_FFN_CHUNKS
    ck2 = 2 * FFN_CHUNK
    row = lambda i, j: (i * nt + j, 0)
    return pl.pallas_call(
        _post_ffn_kernel,
        grid=(n_seq // bs, nt),
        in_specs=[pl.BlockSpec((bs, bt, d), lambda i, j: (i, j, 0)),
                  pl.BlockSpec((m, D_ATT), row),
                  pl.BlockSpec((m, D_CONV), row),
                  pl.BlockSpec((bs, N_MOD, d), lambda i, j: (i, 0, 0)),
                  pl.BlockSpec((nc, bs, k1, ck2), lambda i, j: (0, i, 0, 0)),
                  _const_spec((d, d)),
                  _const_spec((1, d)), _const_spec((1, d)), _const_spec((1, d)),
                  _const_spec((nc, d, ck2)),
                  _const_spec((nc, FFN_CONV_WIDTH, ck2)),
                  _const_spec((nc, 1, ck2)),
                  _const_spec((nc, FFN_CHUNK, d))],
        out_specs=[pl.BlockSpec((bs, bt, d), lambda i, j: (i, j, 0)),
                   pl.BlockSpec((nc, bs, k1, ck2), lambda i, j: (0, i, 0, 0))],
        out_shape=[jax.ShapeDtypeStruct((n_seq, t, d), F32),
                   jax.ShapeDtypeStruct((nc, n_seq, k1, ck2), F32)],
        scratch_shapes=[pltpu.VMEM((nc, bs, SUBLANES, ck2), F32),
                        pltpu.VMEM((2, bs, SUBLANES + bt, ck2), F32),
                        pltpu.VMEM((m, d), BF16),
                        pltpu.VMEM((nc, m, FFN_CHUNK), BF16)],
        compiler_params=_params(2),
        name="post_ffn",
    )(x3, o_att, o_conv, mod, hist_c, w_out, g_post_mix, g_pre_ffn, g_post_ffn,
      w_up_c, cw_c, cb_c, w_down_c)


def _chunk_cols(a):
    lead = a.shape[:-1]
    a = a.reshape(lead + (2, N_FFN_CHUNKS, FFN_CHUNK))
    a = jnp.moveaxis(a, -2, 0)
    return a.reshape((N_FFN_CHUNKS,) + lead + (2 * FFN_CHUNK,))


def _unchunk_cols(a):
    lead = a.shape[1:-1]
    a = a.reshape((N_FFN_CHUNKS,) + lead + (2, FFN_CHUNK))
    a = jnp.moveaxis(a, 0, -2)
    return a.reshape(lead + (2 * D_FF,))


PROMPT_TILES = dict(mix=(1, 512, True), conv=(1, 256), ffn=(1, 512), tq=512)
SAMPLE_TILES = dict(mix=(32, 8, False), conv=(16, 8), ffn=(16, 8), n_grp=16)


def _layer(x3, mod, conv_hist, ffn_hist, attend, tiles, lw):
    n_seq, t, d = x3.shape
    q, kb, vb, k, v, lf, z = _mix_in(x3, mod, lw["g_pre_mix"], lw["w_in_r"], lw["bf_pad"], *tiles["mix"])
    o_att = attend(q, kb, vb, k, v, lf)
    o_conv, conv_state = _conv(z.reshape(n_seq, t, D_CONV), conv_hist, lw["w_dw"], lw["b_dw"],
                               lw["ln_g"], lw["ln_b"], *tiles["conv"])
    y, ffn_state_c = _post_ffn(x3, o_att, o_conv.reshape(n_seq * t, D_CONV), mod, _chunk_cols(ffn_hist),
                               lw["w_out"], lw["g_post_mix"], lw["g_pre_ffn"], lw["g_post_ffn"],
                               lw["w_up_c"], lw["cw_c"], lw["cb_c"], lw["w_down_c"], *tiles["ffn"])
    if tiles["mix"][2]:
        k4 = k.reshape(n_seq, N_HEADS, HEAD_DIM, t).transpose(0, 3, 1, 2)
        v4 = v.reshape(n_seq, N_HEADS, HEAD_DIM, t).transpose(0, 3, 1, 2)
        lf3 = lf.transpose(0, 2, 1)
    else:
        k4 = k.reshape(n_seq, t, N_HEADS, HEAD_DIM)
        v4 = v.reshape(n_seq, t, N_HEADS, HEAD_DIM)
        lf3 = lf.reshape(n_seq, t, N_HEADS)
    return y, k4, v4, lf3, conv_state, _unchunk_cols(ffn_state_c)


def kernel(x_prompt, x_sample, cache_k, cache_v, cache_logf, state_conv, state_ffn, page_table, c_prompt, c_sample, w_ada, b_ada, g_pre_mix, g_post_mix, g_pre_ffn, g_post_ffn, w_in, b_f, w_dw, b_dw, ln_g, ln_b, w_out, w_up, w_ffn_dw, b_ffn_dw, w_down):
    depth = w_ada.shape[0]
    b, t, d = x_prompt.shape
    s_, t_new, _ = x_sample.shape
    n_phys = cache_k.shape[1]
    yp, ys = x_prompt, x_sample
    outs_p, outs_s = [], []
    c_all = jnp.concatenate([c_prompt, c_sample], axis=0)
    c_all = jnp.pad(c_all, ((0, (-c_all.shape[0]) % SUBLANES), (0, 0)))
    for l in range(depth):
        row = lambda a: a[l].reshape(1, -1)
        wi = w_in[l]
        w_in_r = jnp.concatenate(
            [wi[:, :3 * D_ATT], wi[:, 3 * D_ATT + N_HEADS:],
             jnp.pad(wi[:, 3 * D_ATT:3 * D_ATT + N_HEADS], ((0, 0), (0, LANES - N_HEADS)))],
            axis=1).astype(BF16)
        lw = dict(
            g_pre_mix=row(g_pre_mix), g_post_mix=row(g_post_mix),
            g_pre_ffn=row(g_pre_ffn), g_post_ffn=row(g_post_ffn),
            w_in_r=w_in_r, bf_pad=jnp.pad(b_f[l], (0, LANES - N_HEADS)).reshape(1, LANES),
            w_dw=w_dw[l], b_dw=b_dw[l], ln_g=ln_g[l], ln_b=ln_b[l],
            w_out=w_out[l].astype(BF16),
            w_up_c=_chunk_cols(w_up[l].astype(BF16)),
            cw_c=_chunk_cols(w_ffn_dw[l]),
            cb_c=_chunk_cols(b_ffn_dw[l].reshape(1, -1)),
            w_down_c=w_down[l].astype(BF16).reshape(N_FFN_CHUNKS, FFN_CHUNK, d),
        )
        mod = _ada(c_all, w_ada[l], b_ada[l])
        mod_p = mod[:b].reshape(b, N_MOD, d)
        mod_s = mod[b:b + s_].reshape(s_, N_MOD, d)

        def attend_p(q, kb, vb, k, v, lf):
            cum = _cumsum_lanes(lf)
            cum_col = cum.reshape(b, N_HEADS // 2, 2, t).transpose(0, 1, 3, 2)
            return _fox_prompt(q, kb, vb, cum_col, b, t, PROMPT_TILES["tq"])

        def attend_s(q, kb, vb, k, v, lf):
            cn = _cumsum_rows(lf.reshape(s_, t_new, N_HEADS))
            o_t = _fox_sample(q.reshape(s_, t_new, D_ATT),
                              k.reshape(s_, t_new, D_ATT).transpose(0, 2, 1),
                              v.reshape(s_, t_new, D_ATT).transpose(0, 2, 1),
                              cn.reshape(s_, t_new * N_HEADS, 1), cn.transpose(0, 2, 1),
                              cache_k.transpose(0, 1, 3, 4, 2), cache_v.transpose(0, 1, 3, 4, 2),
                              cache_logf.transpose(0, 1, 3, 2), page_table, l, SAMPLE_TILES["n_grp"])
            return o_t.transpose(0, 2, 1).reshape(s_ * t_new, D_ATT)

        conv0 = jnp.zeros((b, CONV_WIDTH - 1, D_CONV), F32)
        ffn0 = jnp.zeros((b, FFN_CONV_WIDTH - 1, 2 * D_FF), F32)
        yp, *rp = _layer(yp, mod_p, conv0, ffn0, attend_p, PROMPT_TILES, lw)
        ys, *rs = _layer(ys, mod_s, state_conv[l], state_ffn[l], attend_s, SAMPLE_TILES, lw)
        outs_p.append(rp)
        outs_s.append(rs)
    stack = lambda outs, i: jnp.stack([o[i] for o in outs])
    return (yp, ys, *[stack(outs_p, i) for i in range(5)], *[stack(outs_s, i) for i in range(5)])
```

```python
import functools

import jax
import jax.numpy as jnp
from jax import lax
from jax.experimental import pallas as pl
from jax.experimental.pallas import tpu as pltpu

F32 = jnp.float32
BF16 = jnp.bfloat16

D_MODEL = 1024
HEAD_DIM = 64
N_HEADS = 8
D_ATT = N_HEADS * HEAD_DIM
D_CONV = D_MODEL - D_ATT
CONV_WIDTH = 31
D_FF = 2816
FFN_CONV_WIDTH = 3
N_MOD = 6
EPS = 1e-6
PAGE_SIZE = 128

LANES = 128
SUBLANES = 8
FFN_CHUNK = 256
N_FFN_CHUNKS = D_FF // FFN_CHUNK
D_IN_PAD = 3 * D_ATT + 2 * D_CONV + LANES
VMEM_LIMIT = 52 * 1024 * 1024


def _sigmoid(x):
    return 1.0 / (1.0 + jnp.exp(-x))


def _rms(x):
    return x * lax.rsqrt(jnp.mean(x * x, axis=-1, keepdims=True) + EPS)


def _params(n_grid):
    return pltpu.CompilerParams(dimension_semantics=("arbitrary",) * n_grid,
                                vmem_limit_bytes=VMEM_LIMIT)


def _const_spec(shape):
    nd = len(shape)
    return pl.BlockSpec(shape, lambda *_: (0,) * nd, pipeline_mode=pl.Buffered(1))


def _ada_kernel(c_ref, w_ref, b_ref, o_ref):
    c = c_ref[...]
    s = (c * _sigmoid(c)).astype(BF16)
    o_ref[...] = jnp.dot(s, w_ref[...].astype(BF16), preferred_element_type=F32) + b_ref[...]


def _ada(c_all, w_ada, b_ada):
    r = c_all.shape[0]
    n = w_ada.shape[1]
    bn = D_MODEL
    return pl.pallas_call(
        _ada_kernel,
        grid=(n // bn,),
        in_specs=[pl.BlockSpec((r, D_MODEL), lambda j: (0, 0)),
                  pl.BlockSpec((D_MODEL, bn), lambda j: (0, j)),
                  pl.BlockSpec((1, bn), lambda j: (0, j))],
        out_specs=pl.BlockSpec((r, bn), lambda j: (0, j)),
        out_shape=jax.ShapeDtypeStruct((r, n), F32),
        compiler_params=_params(1),
        name="ada",
    )(c_all, w_ada, b_ada.reshape(1, n))


def _mix_in_kernel(x_ref, mod_ref, g_ref, w_ref, bf_ref,
                   q_ref, kb_ref, vb_ref, k_ref, v_ref, lf_ref, z_ref, *, time_minor):
    x = x_ref[...]
    bs, bt, d = x.shape
    h = _rms(x) * g_ref[...] * (1.0 + mod_ref[:, 1:2, :]) + mod_ref[:, 0:1, :]
    h = h.reshape(bs * bt, d).astype(BF16)
    p = jnp.dot(h, w_ref[...], preferred_element_type=F32)
    q = p[:, 0:D_ATT]
    k = p[:, D_ATT:2 * D_ATT]
    v = p[:, 2 * D_ATT:3 * D_ATT]
    a = p[:, 3 * D_ATT:3 * D_ATT + D_CONV]
    g = p[:, 3 * D_ATT + D_CONV:3 * D_ATT + 2 * D_CONV]
    f = p[:, 3 * D_ATT + 2 * D_CONV:] + bf_ref[...]
    q_ref[...] = (q * (HEAD_DIM ** -0.5)).astype(BF16)
    kb_ref[...] = k.astype(BF16)
    vb_ref[...] = v.astype(BF16)
    lf = jnp.minimum(f, 0.0) - jnp.log(1.0 + jnp.exp(-jnp.abs(f)))
    if time_minor:
        k_ref[...] = k.T
        v_ref[...] = v.T
        lf_ref[...] = lf.T[0:N_HEADS, :]
    else:
        k_ref[...] = k
        v_ref[...] = v
        lf_ref[...] = lf[:, 0:N_HEADS]
    z_ref[...] = a * _sigmoid(g)


def _mix_in(x3, mod, g_pre, w_in_r, bf_pad, bs, bt, time_minor):
    n_seq, t, d = x3.shape
    n_tok = n_seq * t
    m = bs * bt
    nt = t // bt
    row = lambda i, j: (i * nt + j, 0)
    tok = lambda c, dt: jax.ShapeDtypeStruct((n_tok, c), dt)
    if time_minor:
        assert bs == 1
        kv_spec = pl.BlockSpec((None, D_ATT, bt), lambda i, j: (i, 0, j))
        kv_shape = jax.ShapeDtypeStruct((n_seq, D_ATT, t), F32)
        lf_spec = pl.BlockSpec((None, N_HEADS, bt), lambda i, j: (i, 0, j))
        lf_shape = jax.ShapeDtypeStruct((n_seq, N_HEADS, t), F32)
    else:
        kv_spec, kv_shape = pl.BlockSpec((m, D_ATT), row), tok(D_ATT, F32)
        lf_spec, lf_shape = pl.BlockSpec((m, N_HEADS), row), tok(N_HEADS, F32)
    return pl.pallas_call(
        functools.partial(_mix_in_kernel, time_minor=time_minor),
        grid=(n_seq // bs, nt),
        in_specs=[pl.BlockSpec((bs, bt, d), lambda i, j: (i, j, 0)),
                  pl.BlockSpec((bs, N_MOD, d), lambda i, j: (i, 0, 0)),
                  _const_spec((1, d)),
                  _const_spec((d, D_IN_PAD)),
                  _const_spec((1, LANES))],
        out_specs=[pl.BlockSpec((m, D_ATT), row)] * 3
                  + [kv_spec, kv_spec, lf_spec, pl.BlockSpec((m, D_CONV), row)],
        out_shape=[tok(D_ATT, BF16), tok(D_ATT, BF16), tok(D_ATT, BF16),
                   kv_shape, kv_shape, lf_shape, tok(D_CONV, F32)],
        compiler_params=_params(2),
        name="mix_in",
    )(x3, mod, g_pre, w_in_r, bf_pad)


def _cumsum_lanes_kernel(x_ref, o_ref):
    x = x_ref[...]
    t = x.shape[-1]
    idx = lax.broadcasted_iota(jnp.int32, x.shape, 1)
    step = 1
    while step < t:
        x = x + jnp.where(idx >= step, pltpu.roll(x, step, axis=1), 0.0)
        step *= 2
    o_ref[...] = x


def _cumsum_lanes(x):
    b, h, t = x.shape
    return pl.pallas_call(
        _cumsum_lanes_kernel,
        grid=(b,),
        in_specs=[pl.BlockSpec((None, h, t), lambda i: (i, 0, 0))],
        out_specs=pl.BlockSpec((None, h, t), lambda i: (i, 0, 0)),
        out_shape=jax.ShapeDtypeStruct((b, h, t), F32),
        compiler_params=_params(1),
        name="cumsum_lanes",
    )(x)


def _cumsum_rows_kernel(x_ref, o_ref):
    x = x_ref[...]
    acc = x[:, 0:1, :]
    rows = [acc]
    for t in range(1, x.shape[1]):
        acc = acc + x[:, t:t + 1, :]
        rows.append(acc)
    o_ref[...] = jnp.concatenate(rows, axis=1)


def _cumsum_rows(x):
    return pl.pallas_call(
        _cumsum_rows_kernel,
        out_shape=jax.ShapeDtypeStruct(x.shape, F32),
        name="cumsum_rows",
    )(x)


def _split3(x):
    hi = x.astype(BF16).astype(F32)
    r = x - hi
    mid = r.astype(BF16).astype(F32)
    lo = (r - mid).astype(BF16).astype(F32)
    return hi, mid, lo


N_BIAS = 3


def _augment(x, own, lane, base, pieces, fill):
    out = jnp.where(own, x, 0.0)
    for i in range(N_BIAS):
        out = jnp.where(lane == base + i, pieces[i], out)
    return jnp.where((lane >= base + N_BIAS) & (lane < base + 2 * N_BIAS), fill, out)


def _fox_prompt_kernel(q_ref, k_ref, v_ref, cs_ref, ci_ref, o_ref,
                       kaug_ref, vt_ref, qt_ref, m_ref, acc_ref, ut_ref, *, tq):
    qi = pl.program_id(2)
    t = k_ref.shape[0]
    nk = t // tq
    lane = lax.broadcasted_iota(jnp.int32, (1, LANES), 1)
    sub_head = lax.broadcasted_iota(jnp.int32, (LANES, 1), 0) // HEAD_DIM

    @pl.when(qi == 0)
    def _():
        def prep(c, _):
            off = pl.multiple_of(c * tq, tq)
            kc = k_ref[pl.ds(off, tq), :].astype(F32)
            vt = v_ref[pl.ds(off, tq), :].astype(F32).T
            for h in range(2):
                own = (lane // HEAD_DIM) == h
                cs = jnp.broadcast_to(cs_ref[pl.ds(off, tq), h:h + 1], (tq, LANES))
                kaug = _augment(kc, own, lane, HEAD_DIM * (1 - h), _split3(cs), 1.0)
                kaug_ref[h, c] = kaug.astype(BF16)
                vt_ref[h, c] = jnp.where(sub_head == h, vt, 1.0).astype(BF16)
            return 0
        lax.fori_loop(0, nk, prep, 0)

    q = q_ref[...].astype(F32)
    for h in range(2):
        own = (lane // HEAD_DIM) == h
        ci = jnp.broadcast_to(ci_ref[:, h:h + 1], (tq, LANES))
        base = HEAD_DIM * (1 - h)
        qaug = _augment(q, own, lane, base + N_BIAS, _split3(ci), 0.0)
        qaug = jnp.where((lane >= base) & (lane < base + N_BIAS), -1.0, qaug)
        qt_ref[h] = qaug.T.astype(BF16)
    m_ref[...] = jnp.full(m_ref.shape, -jnp.inf, F32)
    acc_ref[...] = jnp.zeros_like(acc_ref)
    key_pos = lax.broadcasted_iota(jnp.int32, (tq, tq), 0)
    qry_pos = lax.broadcasted_iota(jnp.int32, (tq, tq), 1)

    def scores(kb):
        return [jnp.dot(kaug_ref[h, kb], qt_ref[h], preferred_element_type=F32) for h in range(2)]

    def block(kb, slot, masked, prefetch):
        uts = [ut_ref[slot, h] for h in range(2)]
        nxt = scores(kb + 1) if prefetch else None
        for h in range(2):
            ut = uts[h]
            if masked:
                ut = jnp.where(key_pos <= qry_pos, ut, -jnp.inf)
            m_old = m_ref[h]
            m_new = jnp.maximum(m_old, jnp.max(ut, axis=0, keepdims=True))
            p = jnp.exp(ut - m_new).astype(BF16)
            m_ref[h] = m_new
            acc_ref[h] = (jnp.exp(m_old - m_new) * acc_ref[h]
                          + jnp.dot(vt_ref[h, kb], p, preferred_element_type=F32))
        if prefetch:
            for h in range(2):
                ut_ref[1 - slot, h] = nxt[h]

    def body(i, _):
        block(2 * i, 0, False, True)
        block(2 * i + 1, 1, False, True)
        return 0

    first = scores(0)
    for h in range(2):
        ut_ref[0, h] = first[h]
    lax.fori_loop(0, qi // 2, body, 0)

    @pl.when(qi % 2 == 0)
    def _():
        block(qi, 0, True, False)

    @pl.when(qi % 2 == 1)
    def _():
        block(qi - 1, 0, False, True)
        block(qi, 1, True, False)

    a0 = acc_ref[0]
    a1 = acc_ref[1]
    ot = jnp.concatenate([a0[0:HEAD_DIM] / a0[HEAD_DIM:HEAD_DIM + 1],
                          a1[HEAD_DIM:] / a1[0:1]], axis=0)
    o_ref[...] = ot.T.astype(o_ref.dtype)


def _fox_prompt(q, kb, vb, cum_col, b, t, tq):
    nq = t // tq
    n_pairs = N_HEADS // 2
    return pl.pallas_call(
        functools.partial(_fox_prompt_kernel, tq=tq),
        grid=(b, n_pairs, nq),
        in_specs=[pl.BlockSpec((tq, LANES), lambda i, p, j: (i * nq + j, p)),
                  pl.BlockSpec((t, LANES), lambda i, p, j: (i, p)),
                  pl.BlockSpec((t, LANES), lambda i, p, j: (i, p)),
                  pl.BlockSpec((None, None, t, 2), lambda i, p, j: (i, p, 0, 0)),
                  pl.BlockSpec((None, None, tq, 2), lambda i, p, j: (i, p, j, 0))],
        out_specs=pl.BlockSpec((tq, LANES), lambda i, p, j: (i * nq + j, p)),
        out_shape=jax.ShapeDtypeStruct((b * t, D_ATT), BF16),
        scratch_shapes=[pltpu.VMEM((2, nq, tq, LANES), BF16),
                        pltpu.VMEM((2, nq, LANES, tq), BF16),
                        pltpu.VMEM((2, LANES, tq), BF16),
                        pltpu.VMEM((2, 1, tq), F32),
                        pltpu.VMEM((2, LANES, tq), F32),
                        pltpu.VMEM((2, 2, tq, tq), F32)],
        compiler_params=_params(3),
        name="fox_prompt",
    )(q, kb, vb, cum_col, cum_col)


def _fox_sample_kernel(pt_ref, q_ref, knt_ref, vnt_ref, cnr_ref, cnc_ref, k_hbm, v_hbm, lf_hbm,
                       o_ref, qbd_ref, m_ref, l_ref, acc_ref, carry_ref, kbuf_ref, vbuf_ref,
                       lfbuf_ref, sem, *, n_grp, n_pages, layer):
    s_id = pl.program_id(0)
    j = pl.program_id(1)
    n_steps = pl.num_programs(1)
    n_total = pl.num_programs(0) * n_steps
    n_slots = kbuf_ref.shape[0]
    ahead = n_slots - 1
    step = s_id * n_steps + j
    slot = step % n_slots

    def page_copies(at_step, dst_slot):
        seq = at_step // n_steps
        jj = at_step % n_steps
        first = seq * n_pages + (n_steps - 1 - jj) * n_grp
        lf_copies, kv_copies = [], []
        for g in range(n_grp):
            page = pt_ref[first + g]
            lf_copies.append(pltpu.make_async_copy(lf_hbm.at[layer, page], lfbuf_ref.at[dst_slot, g],
                                                   sem.at[2, dst_slot]))
            kv_copies.append(pltpu.make_async_copy(k_hbm.at[layer, page], kbuf_ref.at[dst_slot, g],
                                                   sem.at[0, dst_slot]))
            kv_copies.append(pltpu.make_async_copy(v_hbm.at[layer, page], vbuf_ref.at[dst_slot, g],
                                                   sem.at[1, dst_slot]))
        return lf_copies, kv_copies

    def start_all(at_step, dst_slot):
        lf_copies, kv_copies = page_copies(at_step, dst_slot)
        for c in lf_copies + kv_copies:
            c.start()

    @pl.when(step == 0)
    def _():
        for first_step in range(ahead):
            start_all(first_step, first_step)

    @pl.when(step + ahead < n_total)
    def _():
        start_all(step + ahead, (step + ahead) % n_slots)

    lf_waits, kv_waits = page_copies(step, slot)
    k_pages = [kbuf_ref.at[slot, g] for g in range(n_grp)]
    v_pages = [vbuf_ref.at[slot, g] for g in range(n_grp)]
    t_new = q_ref.shape[0]
    n_rows = t_new * N_HEADS
    n_keys = n_grp * PAGE_SIZE
    nt_dims = (((1,), (1,)), ((), ()))
    head_of_lane = lax.broadcasted_iota(jnp.int32, (N_HEADS, D_ATT), 1) // HEAD_DIM
    head_mask = (head_of_lane == lax.broadcasted_iota(jnp.int32, (N_HEADS, D_ATT), 0)).astype(F32)

    @pl.when(j == 0)
    def _():
        q = q_ref[...].astype(F32)
        qbd = (q[:, None, :] * head_mask[None]).reshape(n_rows, D_ATT)
        qbd_ref[...] = qbd.astype(BF16)
        s = jnp.dot(qbd, knt_ref[...], preferred_element_type=F32)
        u = (s.reshape(t_new, N_HEADS, t_new) - cnc_ref[...][None]).reshape(n_rows, t_new)
        u = u + cnr_ref[...]
        tok = lax.broadcasted_iota(jnp.int32, (n_rows, t_new), 0) // N_HEADS
        key = lax.broadcasted_iota(jnp.int32, (n_rows, t_new), 1)
        u = jnp.where(key <= tok, u, -jnp.inf)
        m = jnp.max(u, axis=1, keepdims=True)
        p = jnp.exp(u - m)
        m_ref[...] = m
        acc_ref[...] = lax.dot_general(vnt_ref[...], p, nt_dims, preferred_element_type=F32)
        l_ref[...] = lax.dot_general(jnp.ones((l_ref.shape[0], t_new), F32), p, nt_dims,
                                     preferred_element_type=F32)
        carry_ref[...] = jnp.zeros_like(carry_ref)

    src = lax.broadcasted_iota(jnp.int32, (PAGE_SIZE, 2 * PAGE_SIZE), 0)
    dst = lax.broadcasted_iota(jnp.int32, (PAGE_SIZE, 2 * PAGE_SIZE), 1)
    later = ((src > dst) | (dst >= PAGE_SIZE)).astype(F32)
    for c in lf_waits:
        c.wait()
    lf = lfbuf_ref[slot].reshape(n_grp * N_HEADS, PAGE_SIZE)
    sums = jnp.dot(lf, later, precision=lax.Precision.HIGHEST, preferred_element_type=F32)
    run = carry_ref[...]
    rests = [None] * n_grp
    for g in reversed(range(n_grp)):
        rows = slice(g * N_HEADS, (g + 1) * N_HEADS)
        rests[g] = sums[rows, 0:PAGE_SIZE] + run
        run = run + sums[rows, PAGE_SIZE:]
    carry_ref[...] = run
    rest_all = jnp.concatenate(rests, axis=1)

    for c in kv_waits:
        c.wait()
    kt = jnp.concatenate([r[...].reshape(D_ATT, PAGE_SIZE).astype(BF16) for r in k_pages], axis=1)
    s = jnp.dot(qbd_ref[...], kt, preferred_element_type=F32)
    u = (s.reshape(t_new, N_HEADS, n_keys) + rest_all[None]).reshape(n_rows, n_keys)
    u = u + cnr_ref[...]
    m_old = m_ref[...]
    m_new = jnp.maximum(m_old, jnp.max(u, axis=1, keepdims=True))
    m_ref[...] = m_new
    p = jnp.exp(u - m_new).astype(BF16)
    eye = (lax.broadcasted_iota(jnp.int32, (n_rows, n_rows), 0)
           == lax.broadcasted_iota(jnp.int32, (n_rows, n_rows), 1))
    alpha = jnp.sum(jnp.where(eye, jnp.exp(m_old - m_new), 0.0), axis=0, keepdims=True)
    vt = jnp.concatenate([r[...].reshape(D_ATT, PAGE_SIZE).astype(BF16) for r in v_pages], axis=1)
    acc_ref[...] = alpha * acc_ref[...] + lax.dot_general(vt, p, nt_dims, preferred_element_type=F32)
    l_ref[...] = alpha * l_ref[...] + lax.dot_general(jnp.ones((l_ref.shape[0], n_keys), BF16), p,
                                                      nt_dims, preferred_element_type=F32)

    @pl.when(j == pl.num_programs(1) - 1)
    def _():
        o_t = acc_ref[...] / l_ref[0:1, :]
        keep = (lax.broadcasted_iota(jnp.int32, o_t.shape, 0) // HEAD_DIM
                == lax.broadcasted_iota(jnp.int32, o_t.shape, 1) % N_HEADS)
        gather = (lax.broadcasted_iota(jnp.int32, (n_rows, t_new), 0) // N_HEADS
                  == lax.broadcasted_iota(jnp.int32, (n_rows, t_new), 1)).astype(F32)
        o_ref[...] = jnp.dot(jnp.where(keep, o_t, 0.0), gather,
                             preferred_element_type=F32).astype(o_ref.dtype)


def _fox_sample(q3, knt, vnt, cn_rows, cn_cols, cache_kt, cache_vt, cache_lft, page_table,
                layer, n_grp, n_slots):
    s_, t_new, _ = q3.shape
    n_rows = t_new * N_HEADS
    n_pages = page_table.shape[1]
    n_steps = n_pages // n_grp

    seq3 = lambda s, j, pt: (s, 0, 0)
    in_specs = [pl.BlockSpec((None, t_new, D_ATT), seq3),
                pl.BlockSpec((None, D_ATT, t_new), seq3),
                pl.BlockSpec((None, D_ATT, t_new), seq3),
                pl.BlockSpec((None, n_rows, 1), seq3),
                pl.BlockSpec((None, N_HEADS, t_new), seq3),
                pl.BlockSpec(memory_space=pl.ANY),
                pl.BlockSpec(memory_space=pl.ANY),
                pl.BlockSpec(memory_space=pl.ANY)]
    page_buf = pltpu.VMEM((n_slots, n_grp, N_HEADS, HEAD_DIM, PAGE_SIZE), F32)
    grid_spec = pltpu.PrefetchScalarGridSpec(
        num_scalar_prefetch=1,
        grid=(s_, n_steps),
        in_specs=in_specs,
        out_specs=pl.BlockSpec((None, D_ATT, t_new), seq3),
        scratch_shapes=[pltpu.VMEM((n_rows, D_ATT), BF16),
                        pltpu.VMEM((n_rows, 1), F32),
                        pltpu.VMEM((2 * SUBLANES, n_rows), F32),
                        pltpu.VMEM((D_ATT, n_rows), F32),
                        pltpu.VMEM((N_HEADS, PAGE_SIZE), F32),
                        page_buf, page_buf,
                        pltpu.VMEM((n_slots, n_grp, N_HEADS, PAGE_SIZE), F32),
                        pltpu.SemaphoreType.DMA((3, n_slots))])
    return pl.pallas_call(
        functools.partial(_fox_sample_kernel, n_grp=n_grp, n_pages=n_pages, layer=layer),
        grid_spec=grid_spec,
        out_shape=jax.ShapeDtypeStruct((s_, D_ATT, t_new), BF16),
        compiler_params=_params(2),
        name="fox_sample",
    )(page_table.reshape(-1), q3, knt, vnt, cn_rows, cn_cols, cache_kt, cache_vt, cache_lft)


HALO = 32


def _conv_kernel(z_ref, hist_ref, w_ref, b_ref, g_ref, bb_ref, o_ref, st_ref, ext_ref, *, bt):
    t = pl.program_id(1)
    pad = HALO - (CONV_WIDTH - 1)

    bs = z_ref.shape[0]

    @pl.when(t == 0)
    def _():
        ext_ref[:, 0:pad, :] = jnp.zeros((bs, pad, D_CONV), F32)
        ext_ref[:, pad:HALO, :] = hist_ref[...]

    @pl.when(t > 0)
    def _():
        ext_ref[:, 0:HALO, :] = ext_ref[:, bt:bt + HALO, :]

    ext_ref[:, HALO:HALO + bt, :] = z_ref[...]
    acc = jnp.zeros((bs, bt, D_CONV), F32) + b_ref[...]
    for r in range(SUBLANES):
        rows = bt + (SUBLANES if r else 0)
        part = None
        for j in range(CONV_WIDTH):
            if (j + pad) % SUBLANES == r:
                term = w_ref[j:j + 1, :] * ext_ref[:, pl.ds(j + pad - r, rows), :]
                part = term if part is None else part + term
        acc = acc + part[:, r:r + bt, :]
    mu = jnp.mean(acc, axis=-1, keepdims=True)
    xc = acc - mu
    var = jnp.mean(xc * xc, axis=-1, keepdims=True)
    y = xc * lax.rsqrt(var + EPS) * g_ref[...] + bb_ref[...]
    o_ref[...] = (y * _sigmoid(y)).astype(o_ref.dtype)

    @pl.when(t == pl.num_programs(1) - 1)
    def _():
        st_ref[...] = ext_ref[:, bt + pad:bt + HALO, :]


def _conv(z3, hist, w_dw, b_dw, ln_g, ln_b, bs, bt):
    n_seq, t, c = z3.shape
    k1 = CONV_WIDTH - 1
    return pl.pallas_call(
        functools.partial(_conv_kernel, bt=bt),
        grid=(n_seq // bs, t // bt),
        in_specs=[pl.BlockSpec((bs, bt, c), lambda i, j: (i, j, 0)),
                  pl.BlockSpec((bs, k1, c), lambda i, j: (i, 0, 0)),
                  _const_spec((CONV_WIDTH, c)),
                  _const_spec((1, c)), _const_spec((1, c)), _const_spec((1, c))],
        out_specs=[pl.BlockSpec((bs, bt, c), lambda i, j: (i, j, 0)),
                   pl.BlockSpec((bs, k1, c), lambda i, j: (i, 0, 0))],
        out_shape=[jax.ShapeDtypeStruct((n_seq, t, c), BF16),
                   jax.ShapeDtypeStruct((n_seq, k1, c), F32)],
        scratch_shapes=[pltpu.VMEM((bs, HALO + bt, c), F32)],
        compiler_params=_params(2),
        name="conv",
    )(z3, hist, w_dw, b_dw.reshape(1, c), ln_g.reshape(1, c), ln_b.reshape(1, c))


def _post_ffn_kernel(x_ref, oa_ref, oc_ref, mod_ref, hist_ref, wo_ref, gpm_ref, gpf_ref, gqf_ref,
                     wu_ref, cw_ref, cb_ref, wd_ref, y_ref, st_ref,
                     carry_ref, ext_ref, h2_ref, g_ref):
    t = pl.program_id(1)
    bs, bt, d = x_ref.shape
    m = bs * bt
    ck = FFN_CHUNK
    k1 = FFN_CONV_WIDTH - 1

    mix = (jnp.dot(oa_ref[...], wo_ref[0:D_ATT, :], preferred_element_type=F32)
           + jnp.dot(oc_ref[...], wo_ref[D_ATT:, :], preferred_element_type=F32))
    x1 = x_ref[...] + mod_ref[:, 2:3, :] * (_rms(mix.reshape(bs, bt, d)) * gpm_ref[...])
    y_ref[...] = x1
    h2 = _rms(x1) * gpf_ref[...] * (1.0 + mod_ref[:, 4:5, :]) + mod_ref[:, 3:4, :]
    h2_ref[...] = h2.reshape(m, d).astype(BF16)

    @pl.when(t == 0)
    def _():
        carry_ref[...] = jnp.zeros_like(carry_ref)
        carry_ref[:, :, SUBLANES - k1:SUBLANES, :] = hist_ref[...]

    def gate_chunk(c, slot):
        up = jnp.dot(h2_ref[...], wu_ref[c], preferred_element_type=F32).reshape(bs, bt, 2 * ck)
        ext = ext_ref.at[slot]
        ext[:, 0:SUBLANES, :] = carry_ref[c]
        ext[:, SUBLANES:, :] = up
        carry_ref[c] = up[:, bt - SUBLANES:, :]
        st_ref[c] = up[:, bt - k1:, :]
        w = cw_ref[c]
        u = (w[0:1, :] * ext[:, pl.ds(SUBLANES - 2, bt), :]
             + w[1:2, :] * ext[:, pl.ds(SUBLANES - 1, bt), :]
             + w[2:3, :] * up + cb_ref[c])
        ua = u[:, :, 0:ck]
        g_ref[c] = (ua * _sigmoid(ua) * u[:, :, ck:]).reshape(m, ck).astype(BF16)

    def body(i, _):
        gate_chunk(2 * i, 0)
        gate_chunk(2 * i + 1, 1)
        return 0

    lax.fori_loop(0, N_FFN_CHUNKS // 2, body, 0)
    for c in range(N_FFN_CHUNKS - N_FFN_CHUNKS % 2, N_FFN_CHUNKS):
        gate_chunk(c, 0)
    ffn = jnp.dot(g_ref[0], wd_ref[0], preferred_element_type=F32)
    for c in range(1, N_FFN_CHUNKS):
        ffn = ffn + jnp.dot(g_ref[c], wd_ref[c], preferred_element_type=F32)
    ffn = ffn.reshape(bs, bt, d)
    y_ref[...] = y_ref[...] + mod_ref[:, 5:6, :] * (_rms(ffn) * gqf_ref[...])


def _post_ffn(x3, o_att, o_conv, mod, hist_c, w_out, g_post_mix, g_pre_ffn, g_post_ffn,
              w_up_c, cw_c, cb_c, w_down_c, bs, bt):
    n_seq, t, d = x3.shape
    m = bs * bt
    nt = t // bt
    k1 = FFN_CONV_WIDTH - 1
    nc = N_FFN_CHUNKS
    ck2 = 2 * FFN_CHUNK
    row = lambda i, j: (i * nt + j, 0)
    return pl.pallas_call(
        _post_ffn_kernel,
        grid=(n_seq // bs, nt),
        in_specs=[pl.BlockSpec((bs, bt, d), lambda i, j: (i, j, 0)),
                  pl.BlockSpec((m, D_ATT), row),
                  pl.BlockSpec((m, D_CONV), row),
                  pl.BlockSpec((bs, N_MOD, d), lambda i, j: (i, 0, 0)),
                  pl.BlockSpec((nc, bs, k1, ck2), lambda i, j: (0, i, 0, 0)),
                  _const_spec((d, d)),
                  _const_spec((1, d)), _const_spec((1, d)), _const_spec((1, d)),
                  _const_spec((nc, d, ck2)),
                  _const_spec((nc, FFN_CONV_WIDTH, ck2)),
                  _const_spec((nc, 1, ck2)),
                  _const_spec((nc, FFN_CHUNK, d))],
        out_specs=[pl.BlockSpec((bs, bt, d), lambda i, j: (i, j, 0)),
                   pl.BlockSpec((nc, bs, k1, ck2), lambda i, j: (0, i, 0, 0))],
        out_shape=[jax.ShapeDtypeStruct((n_seq, t, d), F32),
                   jax.ShapeDtypeStruct((nc, n_seq, k1, ck2), F32)],
        scratch_shapes=[pltpu.VMEM((nc, bs, SUBLANES, ck2), F32),
                        pltpu.VMEM((2, bs, SUBLANES + bt, ck2), F32),
                        pltpu.VMEM((m, d), BF16),
                        pltpu.VMEM((nc, m, FFN_CHUNK), BF16)],
        compiler_params=_params(2),
        name="post_ffn",
    )(x3, o_att, o_conv, mod, hist_c, w_out, g_post_mix, g_pre_ffn, g_post_ffn,
      w_up_c, cw_c, cb_c, w_down_c)


def _chunk_cols(a):
    lead = a.shape[:-1]
    a = a.reshape(lead + (2, N_FFN_CHUNKS, FFN_CHUNK))
    a = jnp.moveaxis(a, -2, 0)
    return a.reshape((N_FFN_CHUNKS,) + lead + (2 * FFN_CHUNK,))


def _unchunk_cols(a):
    lead = a.shape[1:-1]
    a = a.reshape((N_FFN_CHUNKS,) + lead + (2, FFN_CHUNK))
    a = jnp.moveaxis(a, 0, -2)
    return a.reshape(lead + (2 * D_FF,))


PROMPT_TILES = dict(mix=(1, 512, True), conv=(1, 256), ffn=(1, 512), tq=512)
SAMPLE_TILES = dict(mix=(32, 8, False), conv=(16, 8), ffn=(16, 8), n_grp=8, n_slots=4)


def _layer(x3, mod, conv_hist, ffn_hist, attend, tiles, lw):
    n_seq, t, d = x3.shape
    q, kb, vb, k, v, lf, z = _mix_in(x3, mod, lw["g_pre_mix"], lw["w_in_r"], lw["bf_pad"], *tiles["mix"])
    o_att = attend(q, kb, vb, k, v, lf)
    o_conv, conv_state = _conv(z.reshape(n_seq, t, D_CONV), conv_hist, lw["w_dw"], lw["b_dw"],
                               lw["ln_g"], lw["ln_b"], *tiles["conv"])
    y, ffn_state_c = _post_ffn(x3, o_att, o_conv.reshape(n_seq * t, D_CONV), mod, _chunk_cols(ffn_hist),
                               lw["w_out"], lw["g_post_mix"], lw["g_pre_ffn"], lw["g_post_ffn"],
                               lw["w_up_c"], lw["cw_c"], lw["cb_c"], lw["w_down_c"], *tiles["ffn"])
    if tiles["mix"][2]:
        k4 = k.reshape(n_seq, N_HEADS, HEAD_DIM, t).transpose(0, 3, 1, 2)
        v4 = v.reshape(n_seq, N_HEADS, HEAD_DIM, t).transpose(0, 3, 1, 2)
        lf3 = lf.transpose(0, 2, 1)
    else:
        k4 = k.reshape(n_seq, t, N_HEADS, HEAD_DIM)
        v4 = v.reshape(n_seq, t, N_HEADS, HEAD_DIM)
        lf3 = lf.reshape(n_seq, t, N_HEADS)
    return y, k4, v4, lf3, conv_state, _unchunk_cols(ffn_state_c)


def kernel(x_prompt, x_sample, cache_k, cache_v, cache_logf, state_conv, state_ffn, page_table, c_prompt, c_sample, w_ada, b_ada, g_pre_mix, g_post_mix, g_pre_ffn, g_post_ffn, w_in, b_f, w_dw, b_dw, ln_g, ln_b, w_out, w_up, w_ffn_dw, b_ffn_dw, w_down):
    depth = w_ada.shape[0]
    b, t, d = x_prompt.shape
    s_, t_new, _ = x_sample.shape
    n_phys = cache_k.shape[1]
    yp, ys = x_prompt, x_sample
    outs_p, outs_s = [], []
    c_all = jnp.concatenate([c_prompt, c_sample], axis=0)
    c_all = jnp.pad(c_all, ((0, (-c_all.shape[0]) % SUBLANES), (0, 0)))
    for l in range(depth):
        row = lambda a: a[l].reshape(1, -1)
        wi = w_in[l]
        w_in_r = jnp.concatenate(
            [wi[:, :3 * D_ATT], wi[:, 3 * D_ATT + N_HEADS:],
             jnp.pad(wi[:, 3 * D_ATT:3 * D_ATT + N_HEADS], ((0, 0), (0, LANES - N_HEADS)))],
            axis=1).astype(BF16)
        lw = dict(
            g_pre_mix=row(g_pre_mix), g_post_mix=row(g_post_mix),
            g_pre_ffn=row(g_pre_ffn), g_post_ffn=row(g_post_ffn),
            w_in_r=w_in_r, bf_pad=jnp.pad(b_f[l], (0, LANES - N_HEADS)).reshape(1, LANES),
            w_dw=w_dw[l], b_dw=b_dw[l], ln_g=ln_g[l], ln_b=ln_b[l],
            w_out=w_out[l].astype(BF16),
            w_up_c=_chunk_cols(w_up[l].astype(BF16)),
            cw_c=_chunk_cols(w_ffn_dw[l]),
            cb_c=_chunk_cols(b_ffn_dw[l].reshape(1, -1)),
            w_down_c=w_down[l].astype(BF16).reshape(N_FFN_CHUNKS, FFN_CHUNK, d),
        )
        mod = _ada(c_all, w_ada[l], b_ada[l])
        mod_p = mod[:b].reshape(b, N_MOD, d)
        mod_s = mod[b:b + s_].reshape(s_, N_MOD, d)

        def attend_p(q, kb, vb, k, v, lf):
            cum = _cumsum_lanes(lf)
            cum_col = cum.reshape(b, N_HEADS // 2, 2, t).transpose(0, 1, 3, 2)
            return _fox_prompt(q, kb, vb, cum_col, b, t, PROMPT_TILES["tq"])

        def attend_s(q, kb, vb, k, v, lf):
            cn = _cumsum_rows(lf.reshape(s_, t_new, N_HEADS))
            o_t = _fox_sample(q.reshape(s_, t_new, D_ATT),
                              k.reshape(s_, t_new, D_ATT).transpose(0, 2, 1),
                              v.reshape(s_, t_new, D_ATT).transpose(0, 2, 1),
                              cn.reshape(s_, t_new * N_HEADS, 1), cn.transpose(0, 2, 1),
                              cache_k.transpose(0, 1, 3, 4, 2), cache_v.transpose(0, 1, 3, 4, 2),
                              cache_logf.transpose(0, 1, 3, 2), page_table, l,
                              SAMPLE_TILES["n_grp"], SAMPLE_TILES["n_slots"])
            return o_t.transpose(0, 2, 1).reshape(s_ * t_new, D_ATT)

        conv0 = jnp.zeros((b, CONV_WIDTH - 1, D_CONV), F32)
        ffn0 = jnp.zeros((b, FFN_CONV_WIDTH - 1, 2 * D_FF), F32)
        yp, *rp = _layer(yp, mod_p, conv0, ffn0, attend_p, PROMPT_TILES, lw)
        ys, *rs = _layer(ys, mod_s, state_conv[l], state_ffn[l], attend_s, SAMPLE_TILES, lw)
        outs_p.append(rp)
        outs_s.append(rs)
    stack = lambda outs, i: jnp.stack([o[i] for o in outs])
    return (yp, ys, *[stack(outs_p, i) for i in range(5)], *[stack(outs_s, i) for i in range(5)])
```

```python
import functools

import jax
import jax.numpy as jnp
from jax import lax
from jax.experimental import pallas as pl
from jax.experimental.pallas import tpu as pltpu

F32 = jnp.float32
BF16 = jnp.bfloat16

D_MODEL = 1024
HEAD_DIM = 64
N_HEADS = 8
D_ATT = N_HEADS * HEAD_DIM
D_CONV = D_MODEL - D_ATT
CONV_WIDTH = 31
D_FF = 2816
FFN_CONV_WIDTH = 3
N_MOD = 6
EPS = 1e-6
PAGE_SIZE = 128

LANES = 128
SUBLANES = 8
FFN_CHUNK = 256
N_FFN_CHUNKS = D_FF // FFN_CHUNK
D_IN_PAD = 3 * D_ATT + 2 * D_CONV + LANES
VMEM_LIMIT = 52 * 1024 * 1024


def _sigmoid(x):
    return 1.0 / (1.0 + jnp.exp(-x))


def _rms(x):
    return x * lax.rsqrt(jnp.mean(x * x, axis=-1, keepdims=True) + EPS)


def _params(n_grid):
    return pltpu.CompilerParams(dimension_semantics=("arbitrary",) * n_grid,
                                vmem_limit_bytes=VMEM_LIMIT)


def _const_spec(shape):
    nd = len(shape)
    return pl.BlockSpec(shape, lambda *_: (0,) * nd, pipeline_mode=pl.Buffered(1))


def _ada_kernel(c_ref, w_ref, b_ref, o_ref):
    c = c_ref[...]
    s = (c * _sigmoid(c)).astype(BF16)
    o_ref[...] = jnp.dot(s, w_ref[...].astype(BF16), preferred_element_type=F32) + b_ref[...]


def _ada(c_all, w_ada, b_ada):
    r = c_all.shape[0]
    n = w_ada.shape[1]
    bn = D_MODEL
    return pl.pallas_call(
        _ada_kernel,
        grid=(n // bn,),
        in_specs=[pl.BlockSpec((r, D_MODEL), lambda j: (0, 0)),
                  pl.BlockSpec((D_MODEL, bn), lambda j: (0, j)),
                  pl.BlockSpec((1, bn), lambda j: (0, j))],
        out_specs=pl.BlockSpec((r, bn), lambda j: (0, j)),
        out_shape=jax.ShapeDtypeStruct((r, n), F32),
        compiler_params=_params(1),
        name="ada",
    )(c_all, w_ada, b_ada.reshape(1, n))


def _mix_in_kernel(x_ref, mod_ref, g_ref, w_ref, bf_ref,
                   q_ref, kb_ref, vb_ref, k_ref, v_ref, lf_ref, z_ref, *, time_minor):
    x = x_ref[...]
    bs, bt, d = x.shape
    h = _rms(x) * g_ref[...] * (1.0 + mod_ref[:, 1:2, :]) + mod_ref[:, 0:1, :]
    h = h.reshape(bs * bt, d).astype(BF16)
    p = jnp.dot(h, w_ref[...], preferred_element_type=F32)
    q = p[:, 0:D_ATT]
    k = p[:, D_ATT:2 * D_ATT]
    v = p[:, 2 * D_ATT:3 * D_ATT]
    a = p[:, 3 * D_ATT:3 * D_ATT + D_CONV]
    g = p[:, 3 * D_ATT + D_CONV:3 * D_ATT + 2 * D_CONV]
    f = p[:, 3 * D_ATT + 2 * D_CONV:] + bf_ref[...]
    q_ref[...] = (q * (HEAD_DIM ** -0.5)).astype(BF16)
    kb_ref[...] = k.astype(BF16)
    vb_ref[...] = v.astype(BF16)
    lf = jnp.minimum(f, 0.0) - jnp.log(1.0 + jnp.exp(-jnp.abs(f)))
    if time_minor:
        k_ref[...] = k.T
        v_ref[...] = v.T
        lf_ref[...] = lf.T[0:N_HEADS, :]
    else:
        k_ref[...] = k
        v_ref[...] = v
        lf_ref[...] = lf[:, 0:N_HEADS]
    z_ref[...] = a * _sigmoid(g)


def _mix_in(x3, mod, g_pre, w_in_r, bf_pad, bs, bt, time_minor):
    n_seq, t, d = x3.shape
    n_tok = n_seq * t
    m = bs * bt
    nt = t // bt
    row = lambda i, j: (i * nt + j, 0)
    tok = lambda c, dt: jax.ShapeDtypeStruct((n_tok, c), dt)
    if time_minor:
        assert bs == 1
        kv_spec = pl.BlockSpec((None, D_ATT, bt), lambda i, j: (i, 0, j))
        kv_shape = jax.ShapeDtypeStruct((n_seq, D_ATT, t), F32)
        lf_spec = pl.BlockSpec((None, N_HEADS, bt), lambda i, j: (i, 0, j))
        lf_shape = jax.ShapeDtypeStruct((n_seq, N_HEADS, t), F32)
    else:
        kv_spec, kv_shape = pl.BlockSpec((m, D_ATT), row), tok(D_ATT, F32)
        lf_spec, lf_shape = pl.BlockSpec((m, N_HEADS), row), tok(N_HEADS, F32)
    return pl.pallas_call(
        functools.partial(_mix_in_kernel, time_minor=time_minor),
        grid=(n_seq // bs, nt),
        in_specs=[pl.BlockSpec((bs, bt, d), lambda i, j: (i, j, 0)),
                  pl.BlockSpec((bs, N_MOD, d), lambda i, j: (i, 0, 0)),
                  _const_spec((1, d)),
                  _const_spec((d, D_IN_PAD)),
                  _const_spec((1, LANES))],
        out_specs=[pl.BlockSpec((m, D_ATT), row)] * 3
                  + [kv_spec, kv_spec, lf_spec, pl.BlockSpec((m, D_CONV), row)],
        out_shape=[tok(D_ATT, BF16), tok(D_ATT, BF16), tok(D_ATT, BF16),
                   kv_shape, kv_shape, lf_shape, tok(D_CONV, F32)],
        compiler_params=_params(2),
        name="mix_in",
    )(x3, mod, g_pre, w_in_r, bf_pad)


def _cumsum_lanes_kernel(x_ref, o_ref):
    x = x_ref[...]
    t = x.shape[-1]
    idx = lax.broadcasted_iota(jnp.int32, x.shape, 1)
    step = 1
    while step < t:
        x = x + jnp.where(idx >= step, pltpu.roll(x, step, axis=1), 0.0)
        step *= 2
    o_ref[...] = x


def _cumsum_lanes(x):
    b, h, t = x.shape
    return pl.pallas_call(
        _cumsum_lanes_kernel,
        grid=(b,),
        in_specs=[pl.BlockSpec((None, h, t), lambda i: (i, 0, 0))],
        out_specs=pl.BlockSpec((None, h, t), lambda i: (i, 0, 0)),
        out_shape=jax.ShapeDtypeStruct((b, h, t), F32),
        compiler_params=_params(1),
        name="cumsum_lanes",
    )(x)


def _cumsum_rows_kernel(x_ref, o_ref):
    x = x_ref[...]
    acc = x[:, 0:1, :]
    rows = [acc]
    for t in range(1, x.shape[1]):
        acc = acc + x[:, t:t + 1, :]
        rows.append(acc)
    o_ref[...] = jnp.concatenate(rows, axis=1)


def _cumsum_rows(x):
    return pl.pallas_call(
        _cumsum_rows_kernel,
        out_shape=jax.ShapeDtypeStruct(x.shape, F32),
        name="cumsum_rows",
    )(x)


def _split3(x):
    hi = x.astype(BF16).astype(F32)
    r = x - hi
    mid = r.astype(BF16).astype(F32)
    lo = (r - mid).astype(BF16).astype(F32)
    return hi, mid, lo


N_BIAS = 3


def _augment(x, own, lane, base, pieces, fill):
    out = jnp.where(own, x, 0.0)
    for i in range(N_BIAS):
        out = jnp.where(lane == base + i, pieces[i], out)
    return jnp.where((lane >= base + N_BIAS) & (lane < base + 2 * N_BIAS), fill, out)


def _fox_prompt_kernel(q_ref, k_ref, v_ref, cs_ref, ci_ref, o_ref,
                       kaug_ref, vt_ref, qt_ref, m_ref, acc_ref, ut_ref, *, tq):
    qi = pl.program_id(2)
    t = k_ref.shape[0]
    nk = t // tq
    lane = lax.broadcasted_iota(jnp.int32, (1, LANES), 1)
    sub_head = lax.broadcasted_iota(jnp.int32, (LANES, 1), 0) // HEAD_DIM

    @pl.when(qi == 0)
    def _():
        def prep(c, _):
            off = pl.multiple_of(c * tq, tq)
            kc = k_ref[pl.ds(off, tq), :].astype(F32)
            vt = v_ref[pl.ds(off, tq), :].astype(F32).T
            for h in range(2):
                own = (lane // HEAD_DIM) == h
                cs = jnp.broadcast_to(cs_ref[pl.ds(off, tq), h:h + 1], (tq, LANES))
                kaug = _augment(kc, own, lane, HEAD_DIM * (1 - h), _split3(cs), 1.0)
                kaug_ref[h, c] = kaug.astype(BF16)
                vt_ref[h, c] = jnp.where(sub_head == h, vt, 1.0).astype(BF16)
            return 0
        lax.fori_loop(0, nk, prep, 0)

    q = q_ref[...].astype(F32)
    for h in range(2):
        own = (lane // HEAD_DIM) == h
        ci = jnp.broadcast_to(ci_ref[:, h:h + 1], (tq, LANES))
        base = HEAD_DIM * (1 - h)
        qaug = _augment(q, own, lane, base + N_BIAS, _split3(ci), 0.0)
        qaug = jnp.where((lane >= base) & (lane < base + N_BIAS), -1.0, qaug)
        qt_ref[h] = qaug.T.astype(BF16)
    m_ref[...] = jnp.full(m_ref.shape, -jnp.inf, F32)
    acc_ref[...] = jnp.zeros_like(acc_ref)
    key_pos = lax.broadcasted_iota(jnp.int32, (tq, tq), 0)
    qry_pos = lax.broadcasted_iota(jnp.int32, (tq, tq), 1)

    def scores(kb):
        return [jnp.dot(kaug_ref[h, kb], qt_ref[h], preferred_element_type=F32) for h in range(2)]

    def block(kb, slot, masked, prefetch):
        uts = [ut_ref[slot, h] for h in range(2)]
        nxt = scores(kb + 1) if prefetch else None
        for h in range(2):
            ut = uts[h]
            if masked:
                ut = jnp.where(key_pos <= qry_pos, ut, -jnp.inf)
            m_old = m_ref[h]
            m_new = jnp.maximum(m_old, jnp.max(ut, axis=0, keepdims=True))
            p = jnp.exp(ut - m_new).astype(BF16)
            m_ref[h] = m_new
            acc_ref[h] = (jnp.exp(m_old - m_new) * acc_ref[h]
                          + jnp.dot(vt_ref[h, kb], p, preferred_element_type=F32))
        if prefetch:
            for h in range(2):
                ut_ref[1 - slot, h] = nxt[h]

    def body(i, _):
        block(2 * i, 0, False, True)
        block(2 * i + 1, 1, False, True)
        return 0

    first = scores(0)
    for h in range(2):
        ut_ref[0, h] = first[h]
    lax.fori_loop(0, qi // 2, body, 0)

    @pl.when(qi % 2 == 0)
    def _():
        block(qi, 0, True, False)

    @pl.when(qi % 2 == 1)
    def _():
        block(qi - 1, 0, False, True)
        block(qi, 1, True, False)

    a0 = acc_ref[0]
    a1 = acc_ref[1]
    ot = jnp.concatenate([a0[0:HEAD_DIM] / a0[HEAD_DIM:HEAD_DIM + 1],
                          a1[HEAD_DIM:] / a1[0:1]], axis=0)
    o_ref[...] = ot.T.astype(o_ref.dtype)


def _fox_prompt(q, kb, vb, cum_col, b, t, tq):
    nq = t // tq
    n_pairs = N_HEADS // 2
    return pl.pallas_call(
        functools.partial(_fox_prompt_kernel, tq=tq),
        grid=(b, n_pairs, nq),
        in_specs=[pl.BlockSpec((tq, LANES), lambda i, p, j: (i * nq + j, p)),
                  pl.BlockSpec((t, LANES), lambda i, p, j: (i, p)),
                  pl.BlockSpec((t, LANES), lambda i, p, j: (i, p)),
                  pl.BlockSpec((None, None, t, 2), lambda i, p, j: (i, p, 0, 0)),
                  pl.BlockSpec((None, None, tq, 2), lambda i, p, j: (i, p, j, 0))],
        out_specs=pl.BlockSpec((tq, LANES), lambda i, p, j: (i * nq + j, p)),
        out_shape=jax.ShapeDtypeStruct((b * t, D_ATT), BF16),
        scratch_shapes=[pltpu.VMEM((2, nq, tq, LANES), BF16),
                        pltpu.VMEM((2, nq, LANES, tq), BF16),
                        pltpu.VMEM((2, LANES, tq), BF16),
                        pltpu.VMEM((2, 1, tq), F32),
                        pltpu.VMEM((2, LANES, tq), F32),
                        pltpu.VMEM((2, 2, tq, tq), F32)],
        compiler_params=_params(3),
        name="fox_prompt",
    )(q, kb, vb, cum_col, cum_col)


def _fox_sample_kernel(pt_ref, q_ref, knt_ref, vnt_ref, cnr_ref, cnc_ref, k_hbm, v_hbm, lf_hbm,
                       o_ref, qbd_ref, m_ref, l_ref, acc_ref, carry_ref, kbuf_ref, vbuf_ref,
                       lfbuf_ref, sem, *, n_grp, n_pages, layer):
    s_id = pl.program_id(0)
    j = pl.program_id(1)
    n_steps = pl.num_programs(1)
    step = s_id * n_steps + j
    slot = step % 2

    def page_copies(seq, jj, dst_slot):
        first = seq * n_pages + (n_steps - 1 - jj) * n_grp
        lf_copies, kv_copies = [], []
        for g in range(n_grp):
            page = pt_ref[first + g]
            lf_copies.append(pltpu.make_async_copy(lf_hbm.at[layer, page], lfbuf_ref.at[dst_slot, g],
                                                   sem.at[2, dst_slot]))
            kv_copies.append(pltpu.make_async_copy(k_hbm.at[layer, page], kbuf_ref.at[dst_slot, g],
                                                   sem.at[0, dst_slot]))
            kv_copies.append(pltpu.make_async_copy(v_hbm.at[layer, page], vbuf_ref.at[dst_slot, g],
                                                   sem.at[1, dst_slot]))
        return lf_copies, kv_copies

    def start_all(seq, jj, dst_slot):
        lf_copies, kv_copies = page_copies(seq, jj, dst_slot)
        for i, c in enumerate(lf_copies + kv_copies):
            c.start(priority=i % 2)

    @pl.when(step == 0)
    def _():
        start_all(s_id, j, slot)

    @pl.when(step + 1 < pl.num_programs(0) * n_steps)
    def _():
        wrap = j + 1 == n_steps
        start_all(jnp.where(wrap, s_id + 1, s_id), jnp.where(wrap, 0, j + 1), 1 - slot)

    lf_waits, kv_waits = page_copies(s_id, j, slot)
    k_pages = [kbuf_ref.at[slot, g] for g in range(n_grp)]
    v_pages = [vbuf_ref.at[slot, g] for g in range(n_grp)]
    t_new = q_ref.shape[0]
    n_rows = t_new * N_HEADS
    n_keys = n_grp * PAGE_SIZE
    nt_dims = (((1,), (1,)), ((), ()))
    head_of_lane = lax.broadcasted_iota(jnp.int32, (N_HEADS, D_ATT), 1) // HEAD_DIM
    head_mask = (head_of_lane == lax.broadcasted_iota(jnp.int32, (N_HEADS, D_ATT), 0)).astype(F32)

    @pl.when(j == 0)
    def _():
        q = q_ref[...].astype(F32)
        qbd = (q[:, None, :] * head_mask[None]).reshape(n_rows, D_ATT)
        qbd_ref[...] = qbd.astype(BF16)
        s = jnp.dot(qbd, knt_ref[...], preferred_element_type=F32)
        u = (s.reshape(t_new, N_HEADS, t_new) - cnc_ref[...][None]).reshape(n_rows, t_new)
        u = u + cnr_ref[...]
        tok = lax.broadcasted_iota(jnp.int32, (n_rows, t_new), 0) // N_HEADS
        key = lax.broadcasted_iota(jnp.int32, (n_rows, t_new), 1)
        u = jnp.where(key <= tok, u, -jnp.inf)
        m = jnp.max(u, axis=1, keepdims=True)
        p = jnp.exp(u - m)
        m_ref[...] = m
        acc_ref[...] = lax.dot_general(vnt_ref[...], p, nt_dims, preferred_element_type=F32)
        l_ref[...] = lax.dot_general(jnp.ones((l_ref.shape[0], t_new), F32), p, nt_dims,
                                     preferred_element_type=F32)
        carry_ref[...] = jnp.zeros_like(carry_ref)

    src = lax.broadcasted_iota(jnp.int32, (PAGE_SIZE, 2 * PAGE_SIZE), 0)
    dst = lax.broadcasted_iota(jnp.int32, (PAGE_SIZE, 2 * PAGE_SIZE), 1)
    later = ((src > dst) | (dst >= PAGE_SIZE)).astype(F32)
    for c in lf_waits:
        c.wait()
    lf = lfbuf_ref[slot].reshape(n_grp * N_HEADS, PAGE_SIZE)
    sums = jnp.dot(lf, later, precision=lax.Precision.HIGHEST, preferred_element_type=F32)
    run = carry_ref[...]
    rests = [None] * n_grp
    for g in reversed(range(n_grp)):
        rows = slice(g * N_HEADS, (g + 1) * N_HEADS)
        rests[g] = sums[rows, 0:PAGE_SIZE] + run
        run = run + sums[rows, PAGE_SIZE:]
    carry_ref[...] = run
    rest_all = jnp.concatenate(rests, axis=1)

    for c in kv_waits:
        c.wait()
    kt = jnp.concatenate([r[...].reshape(D_ATT, PAGE_SIZE).astype(BF16) for r in k_pages], axis=1)
    s = jnp.dot(qbd_ref[...], kt, preferred_element_type=F32)
    u = (s.reshape(t_new, N_HEADS, n_keys) + rest_all[None]).reshape(n_rows, n_keys)
    u = u + cnr_ref[...]
    m_old = m_ref[...]
    m_new = jnp.maximum(m_old, jnp.max(u, axis=1, keepdims=True))
    m_ref[...] = m_new
    p = jnp.exp(u - m_new).astype(BF16)
    eye = (lax.broadcasted_iota(jnp.int32, (n_rows, n_rows), 0)
           == lax.broadcasted_iota(jnp.int32, (n_rows, n_rows), 1))
    alpha = jnp.sum(jnp.where(eye, jnp.exp(m_old - m_new), 0.0), axis=0, keepdims=True)
    vt = jnp.concatenate([r[...].reshape(D_ATT, PAGE_SIZE).astype(BF16) for r in v_pages], axis=1)
    acc_ref[...] = alpha * acc_ref[...] + lax.dot_general(vt, p, nt_dims, preferred_element_type=F32)
    l_ref[...] = alpha * l_ref[...] + lax.dot_general(jnp.ones((l_ref.shape[0], n_keys), BF16), p,
                                                      nt_dims, preferred_element_type=F32)

    @pl.when(j == pl.num_programs(1) - 1)
    def _():
        o_t = acc_ref[...] / l_ref[0:1, :]
        keep = (lax.broadcasted_iota(jnp.int32, o_t.shape, 0) // HEAD_DIM
                == lax.broadcasted_iota(jnp.int32, o_t.shape, 1) % N_HEADS)
        gather = (lax.broadcasted_iota(jnp.int32, (n_rows, t_new), 0) // N_HEADS
                  == lax.broadcasted_iota(jnp.int32, (n_rows, t_new), 1)).astype(F32)
        o_ref[...] = jnp.dot(jnp.where(keep, o_t, 0.0), gather,
                             preferred_element_type=F32).astype(o_ref.dtype)


def _fox_sample(q3, knt, vnt, cn_rows, cn_cols, cache_kt, cache_vt, cache_lft, page_table,
                layer, n_grp):
    s_, t_new, _ = q3.shape
    n_rows = t_new * N_HEADS
    n_pages = page_table.shape[1]
    n_steps = n_pages // n_grp

    seq3 = lambda s, j, pt: (s, 0, 0)
    in_specs = [pl.BlockSpec((None, t_new, D_ATT), seq3),
                pl.BlockSpec((None, D_ATT, t_new), seq3),
                pl.BlockSpec((None, D_ATT, t_new), seq3),
                pl.BlockSpec((None, n_rows, 1), seq3),
                pl.BlockSpec((None, N_HEADS, t_new), seq3),
                pl.BlockSpec(memory_space=pl.ANY),
                pl.BlockSpec(memory_space=pl.ANY),
                pl.BlockSpec(memory_space=pl.ANY)]
    page_buf = pltpu.VMEM((2, n_grp, N_HEADS, HEAD_DIM, PAGE_SIZE), F32)
    grid_spec = pltpu.PrefetchScalarGridSpec(
        num_scalar_prefetch=1,
        grid=(s_, n_steps),
        in_specs=in_specs,
        out_specs=pl.BlockSpec((None, D_ATT, t_new), seq3),
        scratch_shapes=[pltpu.VMEM((n_rows, D_ATT), BF16),
                        pltpu.VMEM((n_rows, 1), F32),
                        pltpu.VMEM((2 * SUBLANES, n_rows), F32),
                        pltpu.VMEM((D_ATT, n_rows), F32),
                        pltpu.VMEM((N_HEADS, PAGE_SIZE), F32),
                        page_buf, page_buf,
                        pltpu.VMEM((2, n_grp, N_HEADS, PAGE_SIZE), F32),
                        pltpu.SemaphoreType.DMA((3, 2))])
    return pl.pallas_call(
        functools.partial(_fox_sample_kernel, n_grp=n_grp, n_pages=n_pages, layer=layer),
        grid_spec=grid_spec,
        out_shape=jax.ShapeDtypeStruct((s_, D_ATT, t_new), BF16),
        compiler_params=_params(2),
        name="fox_sample",
    )(page_table.reshape(-1), q3, knt, vnt, cn_rows, cn_cols, cache_kt, cache_vt, cache_lft)


HALO = 32


def _conv_kernel(z_ref, hist_ref, w_ref, b_ref, g_ref, bb_ref, o_ref, st_ref, ext_ref, *, bt):
    t = pl.program_id(1)
    pad = HALO - (CONV_WIDTH - 1)

    bs = z_ref.shape[0]

    @pl.when(t == 0)
    def _():
        ext_ref[:, 0:pad, :] = jnp.zeros((bs, pad, D_CONV), F32)
        ext_ref[:, pad:HALO, :] = hist_ref[...]

    @pl.when(t > 0)
    def _():
        ext_ref[:, 0:HALO, :] = ext_ref[:, bt:bt + HALO, :]

    ext_ref[:, HALO:HALO + bt, :] = z_ref[...]
    acc = jnp.zeros((bs, bt, D_CONV), F32) + b_ref[...]
    for r in range(SUBLANES):
        rows = bt + (SUBLANES if r else 0)
        part = None
        for j in range(CONV_WIDTH):
            if (j + pad) % SUBLANES == r:
                term = w_ref[j:j + 1, :] * ext_ref[:, pl.ds(j + pad - r, rows), :]
                part = term if part is None else part + term
        acc = acc + part[:, r:r + bt, :]
    mu = jnp.mean(acc, axis=-1, keepdims=True)
    xc = acc - mu
    var = jnp.mean(xc * xc, axis=-1, keepdims=True)
    y = xc * lax.rsqrt(var + EPS) * g_ref[...] + bb_ref[...]
    o_ref[...] = (y * _sigmoid(y)).astype(o_ref.dtype)

    @pl.when(t == pl.num_programs(1) - 1)
    def _():
        st_ref[...] = ext_ref[:, bt + pad:bt + HALO, :]


def _conv(z3, hist, w_dw, b_dw, ln_g, ln_b, bs, bt):
    n_seq, t, c = z3.shape
    k1 = CONV_WIDTH - 1
    return pl.pallas_call(
        functools.partial(_conv_kernel, bt=bt),
        grid=(n_seq // bs, t // bt),
        in_specs=[pl.BlockSpec((bs, bt, c), lambda i, j: (i, j, 0)),
                  pl.BlockSpec((bs, k1, c), lambda i, j: (i, 0, 0)),
                  _const_spec((CONV_WIDTH, c)),
                  _const_spec((1, c)), _const_spec((1, c)), _const_spec((1, c))],
        out_specs=[pl.BlockSpec((bs, bt, c), lambda i, j: (i, j, 0)),
                   pl.BlockSpec((bs, k1, c), lambda i, j: (i, 0, 0))],
        out_shape=[jax.ShapeDtypeStruct((n_seq, t, c), BF16),
                   jax.ShapeDtypeStruct((n_seq, k1, c), F32)],
        scratch_shapes=[pltpu.VMEM((bs, HALO + bt, c), F32)],
        compiler_params=_params(2),
        name="conv",
    )(z3, hist, w_dw, b_dw.reshape(1, c), ln_g.reshape(1, c), ln_b.reshape(1, c))


def _post_ffn_kernel(x_ref, oa_ref, oc_ref, mod_ref, hist_ref, wo_ref, gpm_ref, gpf_ref, gqf_ref,
                     wu_ref, cw_ref, cb_ref, wd_ref, y_ref, st_ref,
                     carry_ref, ext_ref, h2_ref, g_ref):
    t = pl.program_id(1)
    bs, bt, d = x_ref.shape
    m = bs * bt
    ck = FFN_CHUNK
    k1 = FFN_CONV_WIDTH - 1

    mix = (jnp.dot(oa_ref[...], wo_ref[0:D_ATT, :], preferred_element_type=F32)
           + jnp.dot(oc_ref[...], wo_ref[D_ATT:, :], preferred_element_type=F32))
    x1 = x_ref[...] + mod_ref[:, 2:3, :] * (_rms(mix.reshape(bs, bt, d)) * gpm_ref[...])
    y_ref[...] = x1
    h2 = _rms(x1) * gpf_ref[...] * (1.0 + mod_ref[:, 4:5, :]) + mod_ref[:, 3:4, :]
    h2_ref[...] = h2.reshape(m, d).astype(BF16)

    @pl.when(t == 0)
    def _():
        carry_ref[...] = jnp.zeros_like(carry_ref)
        carry_ref[:, :, SUBLANES - k1:SUBLANES, :] = hist_ref[...]

    def gate_chunk(c, slot):
        up = jnp.dot(h2_ref[...], wu_ref[c], preferred_element_type=F32).reshape(bs, bt, 2 * ck)
        ext = ext_ref.at[slot]
        ext[:, 0:SUBLANES, :] = carry_ref[c]
        ext[:, SUBLANES:, :] = up
        carry_ref[c] = up[:, bt - SUBLANES:, :]
        st_ref[c] = up[:, bt - k1:, :]
        w = cw_ref[c]
        u = (w[0:1, :] * ext[:, pl.ds(SUBLANES - 2, bt), :]
             + w[1:2, :] * ext[:, pl.ds(SUBLANES - 1, bt), :]
             + w[2:3, :] * up + cb_ref[c])
        ua = u[:, :, 0:ck]
        g_ref[c] = (ua * _sigmoid(ua) * u[:, :, ck:]).reshape(m, ck).astype(BF16)

    def body(i, _):
        gate_chunk(2 * i, 0)
        gate_chunk(2 * i + 1, 1)
        return 0

    lax.fori_loop(0, N_FFN_CHUNKS // 2, body, 0)
    for c in range(N_FFN_CHUNKS - N_FFN_CHUNKS % 2, N_FFN_CHUNKS):
        gate_chunk(c, 0)
    ffn = jnp.dot(g_ref[0], wd_ref[0], preferred_element_type=F32)
    for c in range(1, N_FFN_CHUNKS):
        ffn = ffn + jnp.dot(g_ref[c], wd_ref[c], preferred_element_type=F32)
    ffn = ffn.reshape(bs, bt, d)
    y_ref[...] = y_ref[...] + mod_ref[:, 5:6, :] * (_rms(ffn) * gqf_ref[...])


def _post_ffn(x3, o_att, o_conv, mod, hist_c, w_out, g_post_mix, g_pre_ffn, g_post_ffn,
              w_up_c, cw_c, cb_c, w_down_c, bs, bt):
    n_seq, t, d = x3.shape
    m = bs * bt
    nt = t // bt
    k1 = FFN_CONV_WIDTH - 1
    nc = N_FFN_CHUNKS
    ck2 = 2 * FFN_CHUNK
    row = lambda i, j: (i * nt + j, 0)
    return pl.pallas_call(
        _post_ffn_kernel,
        grid=(n_seq // bs, nt),
        in_specs=[pl.BlockSpec((bs, bt, d), lambda i, j: (i, j, 0)),
                  pl.BlockSpec((m, D_ATT), row),
                  pl.BlockSpec((m, D_CONV), row),
                  pl.BlockSpec((bs, N_MOD, d), lambda i, j: (i, 0, 0)),
                  pl.BlockSpec((nc, bs, k1, ck2), lambda i, j: (0, i, 0, 0)),
                  _const_spec((d, d)),
                  _const_spec((1, d)), _const_spec((1, d)), _const_spec((1, d)),
                  _const_spec((nc, d, ck2)),
                  _const_spec((nc, FFN_CONV_WIDTH, ck2)),
                  _const_spec((nc, 1, ck2)),
                  _const_spec((nc, FFN_CHUNK, d))],
        out_specs=[pl.BlockSpec((bs, bt, d), lambda i, j: (i, j, 0)),
                   pl.BlockSpec((nc, bs, k1, ck2), lambda i, j: (0, i, 0, 0))],
        out_shape=[jax.ShapeDtypeStruct((n_seq, t, d), F32),
                   jax.ShapeDtypeStruct((nc, n_seq, k1, ck2), F32)],
        scratch_shapes=[pltpu.VMEM((nc, bs, SUBLANES, ck2), F32),
                        pltpu.VMEM((2, bs, SUBLANES + bt, ck2), F32),
                        pltpu.VMEM((m, d), BF16),
                        pltpu.VMEM((nc, m, FFN_CHUNK), BF16)],
        compiler_params=_params(2),
        name="post_ffn",
    )(x3, o_att, o_conv, mod, hist_c, w_out, g_post_mix, g_pre_ffn, g_post_ffn,
      w_up_c, cw_c, cb_c, w_down_c)


def _chunk_cols(a):
    lead = a.shape[:-1]
    a = a.reshape(lead + (2, N_FFN_CHUNKS, FFN_CHUNK))
    a = jnp.moveaxis(a, -2, 0)
    return a.reshape((N_FFN_CHUNKS,) + lead + (2 * FFN_CHUNK,))


def _unchunk_cols(a):
    lead = a.shape[1:-1]
    a = a.reshape((N_FFN_CHUNKS,) + lead + (2, FFN_CHUNK))
    a = jnp.moveaxis(a, 0, -2)
    return a.reshape(lead + (2 * D_FF,))


PROMPT_TILES = dict(mix=(1, 512, True), conv=(1, 256), ffn=(1, 512), tq=512)
SAMPLE_TILES = dict(mix=(32, 8, False), conv=(16, 8), ffn=(16, 8), n_grp=16)


def _layer(x3, mod, conv_hist, ffn_hist, attend, tiles, lw):
    n_seq, t, d = x3.shape
    q, kb, vb, k, v, lf, z = _mix_in(x3, mod, lw["g_pre_mix"], lw["w_in_r"], lw["bf_pad"], *tiles["mix"])
    o_att = attend(q, kb, vb, k, v, lf)
    o_conv, conv_state = _conv(z.reshape(n_seq, t, D_CONV), conv_hist, lw["w_dw"], lw["b_dw"],
                               lw["ln_g"], lw["ln_b"], *tiles["conv"])
    y, ffn_state_c = _post_ffn(x3, o_att, o_conv.reshape(n_seq * t, D_CONV), mod, _chunk_cols(ffn_hist),
                               lw["w_out"], lw["g_post_mix"], lw["g_pre_ffn"], lw["g_post_ffn"],
                               lw["w_up_c"], lw["cw_c"], lw["cb_c"], lw["w_down_c"], *tiles["ffn"])
    if tiles["mix"][2]:
        k4 = k.reshape(n_seq, N_HEADS, HEAD_DIM, t).transpose(0, 3, 1, 2)
        v4 = v.reshape(n_seq, N_HEADS, HEAD_DIM, t).transpose(0, 3, 1, 2)
        lf3 = lf.transpose(0, 2, 1)
    else:
        k4 = k.reshape(n_seq, t, N_HEADS, HEAD_DIM)
        v4 = v.reshape(n_seq, t, N_HEADS, HEAD_DIM)
        lf3 = lf.reshape(n_seq, t, N_HEADS)
    return y, k4, v4, lf3, conv_state, _unchunk_cols(ffn_state_c)


def kernel(x_prompt, x_sample, cache_k, cache_v, cache_logf, state_conv, state_ffn, page_table, c_prompt, c_sample, w_ada, b_ada, g_pre_mix, g_post_mix, g_pre_ffn, g_post_ffn, w_in, b_f, w_dw, b_dw, ln_g, ln_b, w_out, w_up, w_ffn_dw, b_ffn_dw, w_down):
    depth = w_ada.shape[0]
    b, t, d = x_prompt.shape
    s_, t_new, _ = x_sample.shape
    n_phys = cache_k.shape[1]
    yp, ys = x_prompt, x_sample
    outs_p, outs_s = [], []
    c_all = jnp.concatenate([c_prompt, c_sample], axis=0)
    c_all = jnp.pad(c_all, ((0, (-c_all.shape[0]) % SUBLANES), (0, 0)))
    for l in range(depth):
        row = lambda a: a[l].reshape(1, -1)
        wi = w_in[l]
        w_in_r = jnp.concatenate(
            [wi[:, :3 * D_ATT], wi[:, 3 * D_ATT + N_HEADS:],
             jnp.pad(wi[:, 3 * D_ATT:3 * D_ATT + N_HEADS], ((0, 0), (0, LANES - N_HEADS)))],
            axis=1).astype(BF16)
        lw = dict(
            g_pre_mix=row(g_pre_mix), g_post_mix=row(g_post_mix),
            g_pre_ffn=row(g_pre_ffn), g_post_ffn=row(g_post_ffn),
            w_in_r=w_in_r, bf_pad=jnp.pad(b_f[l], (0, LANES - N_HEADS)).reshape(1, LANES),
            w_dw=w_dw[l], b_dw=b_dw[l], ln_g=ln_g[l], ln_b=ln_b[l],
            w_out=w_out[l].astype(BF16),
            w_up_c=_chunk_cols(w_up[l].astype(BF16)),
            cw_c=_chunk_cols(w_ffn_dw[l]),
            cb_c=_chunk_cols(b_ffn_dw[l].reshape(1, -1)),
            w_down_c=w_down[l].astype(BF16).reshape(N_FFN_CHUNKS, FFN_CHUNK, d),
        )
        mod = _ada(c_all, w_ada[l], b_ada[l])
        mod_p = mod[:b].reshape(b, N_MOD, d)
        mod_s = mod[b:b + s_].reshape(s_, N_MOD, d)

        def attend_p(q, kb, vb, k, v, lf):
            cum = _cumsum_lanes(lf)
            cum_col = cum.reshape(b, N_HEADS // 2, 2, t).transpose(0, 1, 3, 2)
            return _fox_prompt(q, kb, vb, cum_col, b, t, PROMPT_TILES["tq"])

        def attend_s(q, kb, vb, k, v, lf):
            cn = _cumsum_rows(lf.reshape(s_, t_new, N_HEADS))
            o_t = _fox_sample(q.reshape(s_, t_new, D_ATT),
                              k.reshape(s_, t_new, D_ATT).transpose(0, 2, 1),
                              v.reshape(s_, t_new, D_ATT).transpose(0, 2, 1),
                              cn.reshape(s_, t_new * N_HEADS, 1), cn.transpose(0, 2, 1),
                              cache_k.transpose(0, 1, 3, 4, 2), cache_v.transpose(0, 1, 3, 4, 2),
                              cache_logf.transpose(0, 1, 3, 2), page_table, l, SAMPLE_TILES["n_grp"])
            return o_t.transpose(0, 2, 1).reshape(s_ * t_new, D_ATT)

        conv0 = jnp.zeros((b, CONV_WIDTH - 1, D_CONV), F32)
        ffn0 = jnp.zeros((b, FFN_CONV_WIDTH - 1, 2 * D_FF), F32)
        yp, *rp = _layer(yp, mod_p, conv0, ffn0, attend_p, PROMPT_TILES, lw)
        ys, *rs = _layer(ys, mod_s, state_conv[l], state_ffn[l], attend_s, SAMPLE_TILES, lw)
        outs_p.append(rp)
        outs_s.append(rs)
    stack = lambda outs, i: jnp.stack([o[i] for o in outs])
    return (yp, ys, *[stack(outs_p, i) for i in range(5)], *[stack(outs_s, i) for i in range(5)])
```
